```python
import jax, jax.numpy as jnp
from jax import lax
import numpy as np

D_MODEL = 1024
BATCH = 4
SEQ = 4096
DEPTH = 4
DEC_BATCH = 32
DEC_SEQ = 8
PAST_LEN = 8192
PAGE_SIZE = 128

HEAD_DIM = 64
MIX_WIDTH = D_MODEL
N_HEADS_ATT = (MIX_WIDTH // 2) // HEAD_DIM
DILATED_BRANCHES = ((128, 1), (512, 4), (2048, 16))
WIN_MAX = 2048
N_HEADS_HGRN = (MIX_WIDTH // 4) // HEAD_DIM
HGRN_KEY_DIM = 128
HGRN_VAL_DIM = HEAD_DIM
N_HEADS_GLA = (MIX_WIDTH // 4) // HEAD_DIM
GLA_VAL_DIM = HEAD_DIM
GLA_KEY_DIM = GLA_VAL_DIM // 2
GLA_GATE_RANK = 16
GLA_GATE_NORM = 16.0
SCAN_CHUNK = 64
D_FF = 2816
CONV_WIDTH = 3
RMS_EPS = 1e-6

ATT_W = N_HEADS_ATT * HEAD_DIM
HGRN_KW = N_HEADS_HGRN * HGRN_KEY_DIM
HGRN_VW = N_HEADS_HGRN * HGRN_VAL_DIM
GLA_KW = N_HEADS_GLA * GLA_KEY_DIM
GLA_VW = N_HEADS_GLA * GLA_VAL_DIM
COL_SIZES = (ATT_W, ATT_W, ATT_W,
             HGRN_KW, HGRN_KW, HGRN_VW, HGRN_VW,
             GLA_KW, GLA_KW, GLA_VW, GLA_GATE_RANK, GLA_VW)
IN_COLS = sum(COL_SIZES)

kernel_name = "hymba_dilated_hgrn2_gla_convffn_step"


def rms_norm(x, g):
    xf = x.astype(jnp.float32)
    y = xf * lax.rsqrt(jnp.mean(xf * xf, axis=-1, keepdims=True) + RMS_EPS)
    return (y * g.astype(jnp.float32)).astype(x.dtype)


def alibi_slopes(n_heads):
    return 2.0 ** (-8.0 * jnp.arange(1, n_heads + 1, dtype=jnp.float32) / n_heads)


def dilated_band_branch(q, k, v, window, dilation, slopes):
    N, S, H, Dh = q.shape
    J = window // dilation
    L = S // dilation
    nb = -(-L // J)
    Lp = nb * J

    def split(t):
        t = t.reshape(N, L, dilation, H, Dh).transpose(0, 2, 1, 3, 4)
        t = jnp.pad(t, ((0, 0), (0, 0), (0, Lp - L), (0, 0), (0, 0)))
        return t.reshape(N, dilation, nb, J, H, Dh)

    def with_prev(t):
        prev = jnp.pad(t, ((0, 0), (0, 0), (1, 0), (0, 0), (0, 0), (0, 0)))[:, :, :-1]
        return jnp.concatenate([prev, t], axis=3)

    qb = split(q)
    kk = with_prev(split(k))
    vv = with_prev(split(v))
    s = jnp.einsum('nrbqhd,nrbkhd->nrbhqk', qb, kk) * (Dh ** -0.5)
    qi = jnp.arange(J)[:, None]
    ki = jnp.arange(2 * J)[None, :]
    dist = J + qi - ki
    blk = jnp.arange(nb)[:, None, None]
    valid = (dist >= 0) & (dist <= J) & ((blk > 0) | (ki >= J))
    bias = -slopes[:, None, None] * (dist * dilation).astype(jnp.float32)
    s = jnp.where(valid[:, None], s + bias, -jnp.inf)
    lse = jax.nn.logsumexp(s, axis=-1)
    o = jnp.einsum('nrbhqk,nrbkhd->nrbqhd', jnp.exp(s - lse[..., None]), vv)
    o = o.reshape(N, dilation, Lp, H, Dh)[:, :, :L].transpose(0, 2, 1, 3, 4).reshape(N, S, H, Dh)
    lse = lse.transpose(0, 1, 2, 4, 3).reshape(N, dilation, Lp, H)[:, :, :L]
    lse = lse.transpose(0, 2, 1, 3).reshape(N, S, H)
    return o, lse


def dilated_gather_branch(q, k_all, v_all, window, dilation, slopes):
    N, T, H, Dh = q.shape
    La = k_all.shape[1]
    J = window // dilation
    j = jnp.arange(J + 1)
    idx = (La - T) + jnp.arange(T)[:, None] - j[None, :] * dilation
    valid = idx >= 0
    idxc = jnp.clip(idx, 0, La - 1)
    kg = k_all[:, idxc]
    vg = v_all[:, idxc]
    s = jnp.einsum('nthd,ntjhd->nhtj', q, kg) * (Dh ** -0.5)
    s = s - slopes[:, None, None] * (j * dilation).astype(jnp.float32)[None, None, :]
    s = jnp.where(valid[None], s, -jnp.inf)
    lse = jax.nn.logsumexp(s, axis=-1)
    o = jnp.einsum('nhtj,ntjhd->nthd', jnp.exp(s - lse[..., None]), vg)
    return o, lse.transpose(0, 2, 1)


def dilated_attention(q, k, v, win):
    slopes = alibi_slopes(q.shape[2])
    qf = q.astype(jnp.float32)
    if win is None:
        kf, vf = k.astype(jnp.float32), v.astype(jnp.float32)
        branches = [dilated_band_branch(qf, kf, vf, w, d, slopes) for (w, d) in DILATED_BRANCHES]
        keep = min(WIN_MAX, k.shape[1])
        new_k, new_v = k[:, -keep:], v[:, -keep:]
    else:
        k_all = jnp.concatenate([win[0].astype(k.dtype), k], axis=1)
        v_all = jnp.concatenate([win[1].astype(v.dtype), v], axis=1)
        kf, vf = k_all.astype(jnp.float32), v_all.astype(jnp.float32)
        branches = [dilated_gather_branch(qf, kf, vf, w, d, slopes) for (w, d) in DILATED_BRANCHES]
        keep = min(WIN_MAX, k_all.shape[1])
        new_k, new_v = k_all[:, -keep:], v_all[:, -keep:]
    outs = jnp.stack([b[0] for b in branches])
    lses = jnp.stack([b[1] for b in branches])
    wts = jax.nn.softmax(lses, axis=0)
    o = jnp.einsum('gnlh,gnlhd->nlhd', wts, outs)
    return o.astype(q.dtype), new_k, new_v


def gated_linear_attention(q, k, v, log_f, s0):
    N, L, H, K = q.shape
    V = v.shape[-1]
    C = SCAN_CHUNK if L % SCAN_CHUNK == 0 else L
    n = L // C

    def chunks(t):
        return t.astype(jnp.float32).reshape(N, n, C, H, -1).transpose(1, 0, 3, 2, 4)

    qs = chunks(q) * (K ** -0.5)
    ks, vs, gs = chunks(k), chunks(v), chunks(log_f)
    causal = jnp.tril(jnp.ones((C, C), dtype=bool))

    def step(S, inp):
        qc, kc, vc, gc = inp
        b = jnp.cumsum(gc, axis=2)
        inter = jnp.einsum('nhtk,nhkv->nhtv', qc * jnp.exp(b), S)
        diff = b[:, :, :, None, :] - b[:, :, None, :, :]
        decay = jnp.where(causal[:, :, None], jnp.exp(jnp.minimum(diff, 0.0)), 0.0)
        scores = jnp.einsum('nhtk,nhsk,nhtsk->nhts', qc, kc, decay)
        intra = jnp.einsum('nhts,nhsv->nhtv', scores, vc)
        b_last = b[:, :, -1:, :]
        S = jnp.exp(b_last[:, :, 0, :, None]) * S + jnp.einsum('nhsk,nhsv->nhkv', kc * jnp.exp(b_last - b), vc)
        return S, inter + intra

    S, o = lax.scan(step, s0.astype(jnp.float32), (qs, ks, vs, gs))
    o = o.transpose(1, 0, 3, 2, 4).reshape(N, L, H, V)
    return o, S.astype(s0.dtype)


def gated_head_norm(o, gate, g):
    N, L, H, V = o.shape
    o = o * lax.rsqrt(jnp.mean(o * o, axis=-1, keepdims=True) + RMS_EPS)
    o = o.reshape(N, L, H * V) * g.astype(jnp.float32) * jax.nn.silu(gate.astype(jnp.float32))
    return o.astype(gate.dtype)


def hgrn_lower_bounds(lb_logits):
    p = jax.nn.softmax(lb_logits.astype(jnp.float32), axis=0)
    c = jnp.cumsum(p, axis=0)
    return c - c[0:1]


def trunk_layer(x, p, lb, win, s_hgrn0, s_gla0, conv_prev):
    N, L, _ = x.shape
    xn = rms_norm(x, p['g_pre_mix'])
    h = xn @ p['w_in']
    pts = np.cumsum(COL_SIZES)[:-1].tolist()
    (q_a, k_a, v_a, q_b, f_b, i_b, og_b, q_c, k_c, v_c, r_c, og_c) = jnp.split(h, pts, axis=-1)

    def heads(t, nh):
        return t.reshape(N, L, nh, -1)

    o_a, win_k, win_v = dilated_attention(heads(q_a, N_HEADS_ATT), heads(k_a, N_HEADS_ATT),
                                          heads(v_a, N_HEADS_ATT), win)
    lbh = lb.reshape(N_HEADS_HGRN, HGRN_KEY_DIM)
    forget = lbh + (1.0 - lbh) * jax.nn.sigmoid(heads(f_b, N_HEADS_HGRN).astype(jnp.float32))
    o_b, s_hgrn = gated_linear_attention(heads(q_b, N_HEADS_HGRN), 1.0 - forget,
                                         heads(i_b, N_HEADS_HGRN), jnp.log(forget), s_hgrn0)
    o_b = gated_head_norm(o_b, og_b, p['g_hgrn'])
    log_g = jax.nn.log_sigmoid((r_c @ p['w_gate_up']).astype(jnp.float32)
                               + p['b_gate_up'].astype(jnp.float32)) / GLA_GATE_NORM
    o_c, s_gla = gated_linear_attention(heads(q_c, N_HEADS_GLA), heads(k_c, N_HEADS_GLA),
                                        heads(v_c, N_HEADS_GLA), heads(log_g, N_HEADS_GLA), s_gla0)
    o_c = gated_head_norm(o_c, og_c, p['g_gla'])

    mix = jnp.concatenate([o_a.reshape(N, L, ATT_W), o_b, o_c], axis=-1) @ p['w_out']
    x = x + rms_norm(mix, p['g_post_mix'])

    xn = rms_norm(x, p['g_pre_ffn'])
    u = xn @ p['w_up']
    ext = jnp.concatenate([conv_prev.astype(u.dtype), u], axis=1)
    c = sum(ext[:, j:j + L] * p['w_conv'][j] for j in range(CONV_WIDTH))
    a, b = jnp.split(c, 2, axis=-1)
    ffn = (jax.nn.gelu(a, approximate=True) * b) @ p['w_down']
    x = x + rms_norm(ffn, p['g_post_ffn'])
    return x, (win_k, win_v, s_hgrn, s_gla, ext[:, -(CONV_WIDTH - 1):])


def setup_inputs(seed: int = 0) -> dict:
    key = jax.random.key(seed)
    ks = jax.random.split(key, 24)

    def nrm(k, shape, scale):
        return jax.random.normal(k, shape, jnp.float32) * scale

    win_len = min(WIN_MAX, PAST_LEN)
    return {
        "x_prompt": nrm(ks[0], (BATCH, SEQ, D_MODEL), 1.0),
        "x_sample": nrm(ks[1], (DEC_BATCH, DEC_SEQ, D_MODEL), 1.0),
        "cache_win_k": nrm(ks[2], (DEPTH, DEC_BATCH, win_len, N_HEADS_ATT, HEAD_DIM), 1.0),
        "cache_win_v": nrm(ks[3], (DEPTH, DEC_BATCH, win_len, N_HEADS_ATT, HEAD_DIM), 1.0),
        "state_hgrn": nrm(ks[4], (DEPTH, DEC_BATCH, N_HEADS_HGRN, HGRN_KEY_DIM, HGRN_VAL_DIM), 0.5),
        "state_gla": nrm(ks[5], (DEPTH, DEC_BATCH, N_HEADS_GLA, GLA_KEY_DIM, GLA_VAL_DIM), 0.5),
        "state_conv": nrm(ks[6], (DEPTH, DEC_BATCH, CONV_WIDTH - 1, 2 * D_FF), 1.0),
        "w_in": nrm(ks[7], (DEPTH, D_MODEL, IN_COLS), D_MODEL ** -0.5),
        "w_gate_up": nrm(ks[8], (DEPTH, GLA_GATE_RANK, GLA_KW), GLA_GATE_RANK ** -0.5),
        "b_gate_up": nrm(ks[9], (DEPTH, GLA_KW), 0.1),
        "w_out": nrm(ks[10], (DEPTH, MIX_WIDTH, D_MODEL), MIX_WIDTH ** -0.5),
        "g_hgrn": 1.0 + nrm(ks[11], (DEPTH, HGRN_VW), 0.02),
        "g_gla": 1.0 + nrm(ks[12], (DEPTH, GLA_VW), 0.02),
        "lb_logits": nrm(ks[13], (DEPTH, HGRN_KW), 0.1),
        "w_up": nrm(ks[14], (DEPTH, D_MODEL, 2 * D_FF), D_MODEL ** -0.5),
        "w_conv": nrm(ks[15], (DEPTH, CONV_WIDTH, 2 * D_FF), 0.5),
        "w_down": nrm(ks[16], (DEPTH, D_FF, D_MODEL), D_FF ** -0.5),
        "g_pre_mix": 1.0 + nrm(ks[17], (DEPTH, D_MODEL), 0.02),
        "g_post_mix": 1.0 + nrm(ks[18], (DEPTH, D_MODEL), 0.02),
        "g_pre_ffn": 1.0 + nrm(ks[19], (DEPTH, D_MODEL), 0.02),
        "g_post_ffn": 1.0 + nrm(ks[20], (DEPTH, D_MODEL), 0.02),
    }


def reference(x_prompt, x_sample, cache_win_k, cache_win_v, state_hgrn, state_gla, state_conv,
              w_in, w_gate_up, b_gate_up, w_out, g_hgrn, g_gla, lb_logits,
              w_up, w_conv, w_down, g_pre_mix, g_post_mix, g_pre_ffn, g_post_ffn):
    lower_bounds = hgrn_lower_bounds(lb_logits)
    dt = x_prompt.dtype
    zero_hgrn = jnp.zeros((BATCH, N_HEADS_HGRN, HGRN_KEY_DIM, HGRN_VAL_DIM), dt)
    zero_gla = jnp.zeros((BATCH, N_HEADS_GLA, GLA_KEY_DIM, GLA_VAL_DIM), dt)
    zero_conv = jnp.zeros((BATCH, CONV_WIDTH - 1, 2 * D_FF), dt)
    yp, ys = x_prompt, x_sample
    p_states, s_states = [], []
    for l in range(DEPTH):
        p = dict(w_in=w_in[l], w_gate_up=w_gate_up[l], b_gate_up=b_gate_up[l], w_out=w_out[l],
                 g_hgrn=g_hgrn[l], g_gla=g_gla[l], w_up=w_up[l], w_conv=w_conv[l], w_down=w_down[l],
                 g_pre_mix=g_pre_mix[l], g_post_mix=g_post_mix[l],
                 g_pre_ffn=g_pre_ffn[l], g_post_ffn=g_post_ffn[l])
        yp, sp = trunk_layer(yp, p, lower_bounds[l], None, zero_hgrn, zero_gla, zero_conv)
        ys, ss = trunk_layer(ys, p, lower_bounds[l], (cache_win_k[l], cache_win_v[l]),
                             state_hgrn[l], state_gla[l], state_conv[l])
        p_states.append(sp)
        s_states.append(ss)

    def stacked(states, i):
        return jnp.stack([st[i] for st in states])

    win_k_prompt, win_v_prompt = stacked(p_states, 0), stacked(p_states, 1)
    win_k_sample, win_v_sample = stacked(s_states, 0), stacked(s_states, 1)
    hgrn_prompt, hgrn_sample = stacked(p_states, 2), stacked(s_states, 2)
    gla_prompt, gla_sample = stacked(p_states, 3), stacked(s_states, 3)
    conv_prompt, conv_sample = stacked(p_states, 4), stacked(s_states, 4)
    return (yp, ys, win_k_prompt, win_v_prompt, win_k_sample, win_v_sample,
            hgrn_prompt, hgrn_sample, gla_prompt, gla_sample, conv_prompt, conv_sample)
```

```python
import functools
import math

import numpy as np
import jax
import jax.numpy as jnp
from jax import lax
from jax.experimental import pallas as pl
from jax.experimental.pallas import tpu as pltpu

F32 = jnp.float32
BF16 = jnp.bfloat16

D_MODEL = 1024
N_LAYERS = 4
HEAD_DIM = 64
N_HEADS_ATT = 8
ATT_W = N_HEADS_ATT * HEAD_DIM
ATT_J = 128
ATT_DILATIONS = (1, 4, 16)
WIN_MAX = 2048
N_HEADS_LIN = 4
HGRN_K = 128
GLA_K = 32
LIN_V = 64
LIN_VW = N_HEADS_LIN * LIN_V
HGRN_KW = N_HEADS_LIN * HGRN_K
GLA_KW = N_HEADS_LIN * GLA_K
GLA_RANK = 16
GLA_GATE_NORM = 16.0
D_FF = 2816
CONV_W = 3
RMS_EPS = 1e-6
IN_COLS = 3856
IN_COLS_PAD = 3968

LANES = 128
SUBLANES = 8
VMEM_LIMIT_CAP = 56 * 1024 * 1024

SCAN_C = 128
NEG_BIG = 1e30


def _vmem_limit(nbytes):
    return int(min(VMEM_LIMIT_CAP, max(32 * 1024 * 1024, nbytes)))


def _rms(x, g):
    ms = jnp.mean(x * x, axis=-1, keepdims=True)
    return x * lax.rsqrt(ms + RMS_EPS) * g


def _dot(a, b):
    return jnp.dot(a, b, preferred_element_type=F32)


def _dot_nt(a, b):
    return lax.dot_general(a, b, (((1,), (1,)), ((), ())), preferred_element_type=F32)


def _norm_matmul_kernel(x_ref, g_ref, w_ref, o_ref):
    xn = _rms(x_ref[...], g_ref[...]).astype(BF16)
    o_ref[...] = _dot(xn, w_ref[...])


def _norm_matmul(x, g, w, tm):
    m, d = x.shape
    n = w.shape[1]
    est = 2 * tm * d * 4 + 2 * d * n * 2 + 2 * tm * n * 4 + (4 << 20)
    return pl.pallas_call(
        _norm_matmul_kernel,
        grid=(m // tm,),
        in_specs=[
            pl.BlockSpec((tm, d), lambda i: (i, 0)),
            pl.BlockSpec((1, d), lambda i: (0, 0)),
            pl.BlockSpec((d, n), lambda i: (0, 0)),
        ],
        out_specs=pl.BlockSpec((tm, n), lambda i: (i, 0)),
        out_shape=jax.ShapeDtypeStruct((m, n), F32),
        compiler_params=pltpu.CompilerParams(
            dimension_semantics=("arbitrary",), vmem_limit_bytes=_vmem_limit(est)),
        name="norm_matmul",
    )(x, g, w)


def _attn_prompt_kernel(slopes_ref, q_ref, k_ref, v_ref, o_ref, m_s, l_s, acc_s, dt_s, *, seq):
    hp = pl.program_id(1)
    j = ATT_J
    qi = lax.broadcasted_iota(jnp.int32, (j, 2 * j), 0)
    ki = lax.broadcasted_iota(jnp.int32, (j, 2 * j), 1)
    d1 = j + qi - ki
    dt_s[1] = jnp.where((d1 >= 0) & (d1 <= j), d1.astype(F32), NEG_BIG)
    d0 = qi - ki
    dt_s[0] = jnp.where((ki < j) & (d0 >= 0), d0.astype(F32), NEG_BIG)
    lane = lax.broadcasted_iota(jnp.int32, (1, LANES), 1)
    half1 = lane >= HEAD_DIM
    slopes = (slopes_ref[2 * hp], slopes_ref[2 * hp + 1])

    def branch(dil, init):
        nb = seq // (j * dil)
        lognb = int(math.log2(nb))

        def rows(start, size):
            return pl.ds(start, size) if dil == 1 else pl.ds(start, size, stride=dil)

        def body(idx, carry):
            b = idx & (nb - 1)
            r = idx >> lognb
            start = b * (j * dil) + r
            first = b == 0
            kstart = jnp.where(first, r, start - j * dil)
            if dil == 1:
                start = pl.multiple_of(start, j)
                kstart = pl.multiple_of(kstart, j)
            dt = dt_s[jnp.where(first, 0, 1)]
            q = q_ref[rows(start, j), :] * (HEAD_DIM ** -0.5)
            kk = k_ref[rows(kstart, 2 * j), :].astype(BF16)
            vv = v_ref[rows(kstart, 2 * j), :].astype(BF16)
            acc_old = None if init else acc_s[rows(start, j), :]
            acc_new = None
            for half in (0, 1):
                hmask = half1 if half else jnp.logical_not(half1)
                qh = jnp.where(hmask, q, 0.0).astype(BF16)
                s = _dot_nt(qh, kk) - (slopes[half] * float(dil)) * dt
                mx = jnp.max(s, axis=-1, keepdims=True)
                if init:
                    m_new = jnp.broadcast_to(mx, (j, LANES))
                else:
                    m_old = m_s[half, rows(start, j), :]
                    m_new = jnp.maximum(m_old, mx)
                p = jnp.exp(s - jnp.concatenate([m_new, m_new], axis=1))
                rs = jnp.sum(p, axis=-1, keepdims=True)
                pv = _dot(p.astype(BF16), vv)
                if init:
                    l_new = jnp.broadcast_to(rs, (j, LANES))
                    contrib = pv
                else:
                    alpha = jnp.exp(m_old - m_new)
                    l_new = alpha * l_s[half, rows(start, j), :] + rs
                    contrib = alpha * acc_old + pv
                m_s[half, rows(start, j), :] = m_new
                l_s[half, rows(start, j), :] = l_new
                acc_new = contrib if half == 0 else jnp.where(half1, contrib, acc_new)
            acc_s[rows(start, j), :] = acc_new
            return carry

        lax.fori_loop(0, nb * dil, body, 0)

    branch(ATT_DILATIONS[0], True)
    for dil in ATT_DILATIONS[1:]:
        branch(dil, False)

    def finish(c, carry):
        rows = pl.ds(pl.multiple_of(c * j, j), j)
        l = jnp.where(half1, l_s[1, rows, :], l_s[0, rows, :])
        o_ref[rows, :] = acc_s[rows, :] / l
        return carry

    lax.fori_loop(0, seq // j, finish, 0)


def _attn_prompt(h, slopes, n_seq, seq):
    npairs = ATT_W // LANES
    blk = (seq, LANES)
    est = 5 * 2 * seq * LANES * 4 + 5 * seq * LANES * 4 + (8 << 20)
    return pl.pallas_call(
        functools.partial(_attn_prompt_kernel, seq=seq),
        grid=(n_seq, npairs),
        in_specs=[
            pl.BlockSpec(memory_space=pltpu.SMEM),
            pl.BlockSpec(blk, lambda n, p: (n, p)),
            pl.BlockSpec(blk, lambda n, p: (n, npairs + p)),
            pl.BlockSpec(blk, lambda n, p: (n, 2 * npairs + p)),
        ],
        out_specs=pl.BlockSpec(blk, lambda n, p: (n, p)),
        out_shape=jax.ShapeDtypeStruct((n_seq * seq, ATT_W), F32),
        scratch_shapes=[
            pltpu.VMEM((2, seq, LANES), F32),
            pltpu.VMEM((2, seq, LANES), F32),
            pltpu.VMEM((seq, LANES), F32),
            pltpu.VMEM((2, ATT_J, 2 * ATT_J), F32),
        ],
        compiler_params=pltpu.CompilerParams(
            dimension_semantics=("arbitrary", "arbitrary"), vmem_limit_bytes=_vmem_limit(est)),
        name="attn_prompt",
    )(slopes, h, h, h)


def _attn_decode_tables(t_new, win):
    cols = win + LANES
    c = np.arange(cols)[None, :]
    i = np.arange(t_new)[:, None]
    delta = win + i - c
    ok = (delta >= 0) & (c < win + t_new)
    mult = np.zeros((t_new, cols), np.float32)
    for dil in ATT_DILATIONS:
        mult += (ok & (delta % dil == 0) & (delta <= ATT_J * dil)).astype(np.float32)
    slopes = 2.0 ** (-8.0 * np.arange(1, N_HEADS_ATT + 1) / N_HEADS_ATT)
    bias = -slopes[:, None, None] * delta[None].astype(np.float64)
    bias = np.where(mult[None] > 0, bias, -NEG_BIG).astype(np.float32)
    bias = bias.reshape(N_HEADS_ATT * t_new, cols)
    mult = np.tile(mult, (N_HEADS_ATT, 1))
    return bias, mult


def _attn_decode_kernel(bias_ref, mult_ref, q_ref, k_ref, v_ref, ck_ref, cv_ref,
                        o_ref, wk_ref, wv_ref, *, t_new, win):
    lane = lax.broadcasted_iota(jnp.int32, (1, ATT_W), 1)
    q = q_ref[...] * (HEAD_DIM ** -0.5)
    k_new = k_ref[...]
    v_new = v_ref[...]
    qs = jnp.concatenate(
        [jnp.where((lane // HEAD_DIM) == h, q, 0.0) for h in range(N_HEADS_ATT)], axis=0
    ).astype(BF16)
    pad = jnp.zeros((LANES - t_new, ATT_W), F32)
    k_tail = jnp.concatenate([k_new, pad], axis=0).astype(BF16)
    v_tail = jnp.concatenate([v_new, pad], axis=0).astype(BF16)
    ck = ck_ref[...]
    cv = cv_ref[...]
    s = jnp.concatenate([_dot_nt(qs, ck.astype(BF16)), _dot_nt(qs, k_tail)], axis=1)
    s = s + bias_ref[...]
    mx = jnp.max(s, axis=-1, keepdims=True)
    p = mult_ref[...] * jnp.exp(s - mx)
    l = jnp.sum(p, axis=-1, keepdims=True)
    pb = p.astype(BF16)
    o_all = (_dot(pb[:, :win], cv.astype(BF16)) + _dot(pb[:, win:], v_tail)) / l
    o = jnp.zeros((t_new, ATT_W), F32)
    for h in range(N_HEADS_ATT):
        o = jnp.where((lane // HEAD_DIM) == h, o_all[h * t_new:(h + 1) * t_new], o)
    o_ref[...] = o
    wk_ref[0:win - t_new, :] = ck[t_new:win]
    wk_ref[win - t_new:win, :] = k_new
    wv_ref[0:win - t_new, :] = cv[t_new:win]
    wv_ref[win - t_new:win, :] = v_new


def _attn_decode(h, cache_k, cache_v, wk_buf, wv_buf, layer, n_seq, t_new, win):
    bias, mult = _attn_decode_tables(t_new, win)
    rows = N_HEADS_ATT * t_new
    cols = win + LANES
    tab = pl.BlockSpec((rows, cols), lambda n: (0, 0))
    tok = lambda c: pl.BlockSpec((t_new, ATT_W), lambda n, c=c: (n, c))
    cache = pl.BlockSpec((None, None, win, ATT_W), lambda n: (layer, n, 0, 0))
    in_specs = [tab, tab, tok(0), tok(1), tok(2), cache, cache]
    args = [jnp.asarray(bias), jnp.asarray(mult), h, h, h, cache_k, cache_v]
    aliases = {}
    if wk_buf is not None:
        in_specs += [pl.BlockSpec(memory_space=pl.ANY), pl.BlockSpec(memory_space=pl.ANY)]
        args += [wk_buf, wv_buf]
        aliases = {7: 1, 8: 2}
    kernel = functools.partial(_attn_decode_kernel, t_new=t_new, win=win)
    if wk_buf is not None:
        kernel = lambda *refs, _k=kernel: _k(*refs[:7], *refs[9:])
    win_shape = jax.ShapeDtypeStruct((N_LAYERS, n_seq, win, ATT_W), F32)
    est = 8 * win * ATT_W * 4 + 4 * win * ATT_W * 4 + (8 << 20)
    return pl.pallas_call(
        kernel,
        grid=(n_seq,),
        in_specs=in_specs,
        out_specs=[pl.BlockSpec((t_new, ATT_W), lambda n: (n, 0)), cache, cache],
        out_shape=[jax.ShapeDtypeStruct((n_seq * t_new, ATT_W), F32), win_shape, win_shape],
        input_output_aliases=aliases,
        compiler_params=pltpu.CompilerParams(
            dimension_semantics=("arbitrary",), vmem_limit_bytes=_vmem_limit(est)),
        name="attn_decode",
    )(*args)


def _scan_tables(c):
    levels = int(math.log2(c))
    t = np.arange(c)[:, None]
    u = np.arange(c)[None, :]
    mats, masks = [], [np.eye(c, dtype=np.float32)]
    for li in range(levels):
        m = 1 << li
        bnd = (t // (2 * m)) * 2 * m + m - 1
        right = (t % (2 * m)) >= m
        mats.append(np.where(right, (u > bnd) & (u <= t), (u > t) & (u <= bnd)))
        masks.append(((t // (2 * m)) == (u // (2 * m))) & right & ((u % (2 * m)) < m))
    mats.append(u <= t)
    mats.append(u > t)
    tall = np.concatenate(mats, axis=0).astype(np.float32)
    return jnp.asarray(tall, BF16), jnp.asarray(np.stack(masks).astype(np.float32))


def _scan_exponents(g, tall_ref, c):
    g1 = g.astype(BF16)
    r1 = g - g1.astype(F32)
    g2 = r1.astype(BF16)
    g3 = (r1 - g2.astype(F32)).astype(BF16)
    arg = _dot(tall_ref[...], jnp.concatenate([g1, g2, g3], axis=1))
    arg = arg[:, :LANES] + arg[:, LANES:2 * LANES] + arg[:, 2 * LANES:]
    return jnp.exp(arg)


def _scan_scores(qs, kk, e, masks_ref, c, qmask=None):
    levels = int(math.log2(c))
    row = lax.broadcasted_iota(jnp.int32, (c, 1), 0)

    def sel(x):
        return x if qmask is None else jnp.where(qmask, x, 0.0)

    a = masks_ref[0] * _dot_nt(sel(qs).astype(BF16), kk.astype(BF16))
    for li in range(levels):
        m = 1 << li
        right = (row & (2 * m - 1)) >= m
        w = jnp.where(right, qs, kk) * e[li * c:(li + 1) * c]
        a = a + masks_ref[li + 1] * _dot_nt(sel(w).astype(BF16), w.astype(BF16))
    return a


def _pad_rows(x, c):
    if x.shape[0] == c:
        return x
    return jnp.concatenate([x, jnp.zeros((c - x.shape[0], x.shape[1]), x.dtype)], axis=0)


def _gated_head_norm(o, gate, g):
    lane = lax.broadcasted_iota(jnp.int32, (1, LANES), 1)
    half1 = lane >= LIN_V
    o2 = o * o
    parts = []
    for p in range(LIN_VW // LANES):
        o2p = o2[:, p * LANES:(p + 1) * LANES]
        s_all = jnp.sum(o2p, axis=-1, keepdims=True)
        s_hi = jnp.sum(jnp.where(half1, o2p, 0.0), axis=-1, keepdims=True)
        parts.append(jnp.where(half1, s_hi, s_all - s_hi))
    ms = jnp.concatenate(parts, axis=1) * (1.0 / LIN_V)
    on = o * lax.rsqrt(ms + RMS_EPS)
    return on * g * (gate * (1.0 / (1.0 + jnp.exp(-gate))))


def _hgrn_kernel(q_ref, f_ref, i_ref, og_ref, lbl_ref, gh_ref, s0_ref, tall_ref, masks_ref,
                 o_ref, sout_ref, st_s, *, layer, c, c_in, n_chunks):
    ci = pl.program_id(1)

    @pl.when(ci == 0)
    def _():
        st_s[...] = s0_ref[0]

    lbl = lbl_ref[...]
    ex = jnp.exp(lbl - jnp.max(lbl, axis=0, keepdims=True))
    prob = ex / jnp.sum(ex, axis=0, keepdims=True)
    lb = jnp.zeros((1, HGRN_KW), F32)
    for r in range(1, layer + 1):
        lb = lb + prob[r:r + 1]

    q = _pad_rows(q_ref[...], c)
    f = _pad_rows(f_ref[...], c)
    vi = _pad_rows(i_ref[...], c)
    forget = lb + (1.0 - lb) * (1.0 / (1.0 + jnp.exp(-f)))
    g = jnp.log(forget)
    k = 1.0 - forget
    if c_in < c:
        valid = lax.broadcasted_iota(jnp.int32, (c, 1), 0) < c_in
        g = jnp.where(valid, g, 0.0)
        k = jnp.where(valid, k, 0.0)

    lane = lax.broadcasted_iota(jnp.int32, (1, LANES), 1)
    half1 = lane >= LIN_V
    levels = int(math.log2(c))
    pairs = []
    for p in range(N_HEADS_LIN // 2):
        v = vi[:, p * LANES:(p + 1) * LANES]
        vb = v.astype(BF16)
        vt = v.T.astype(BF16)
        o_pair = None
        for half in range(2):
            hd = 2 * p + half
            cols = slice(hd * HGRN_K, (hd + 1) * HGRN_K)
            qs = q[:, cols] * (HGRN_K ** -0.5)
            kk = k[:, cols]
            e = _scan_exponents(g[:, cols], tall_ref, c)
            a = _scan_scores(qs, kk, e, masks_ref, c)
            eb = e[levels * c:(levels + 1) * c]
            eu = e[(levels + 1) * c:]
            st = st_s[hd]
            o_h = _dot(a.astype(BF16), vb) + _dot_nt((qs * eb).astype(BF16), st.astype(BF16))
            st_s[hd] = st * eb[c - 1:c, :] + _dot(vt, (kk * eu).astype(BF16))
            o_pair = o_h if half == 0 else jnp.where(half1, o_h, o_pair)
        pairs.append(o_pair)
    o = jnp.concatenate(pairs, axis=1)
    out = _gated_head_norm(o, _pad_rows(og_ref[...], c), gh_ref[...])
    o_ref[...] = out[:c_in]

    @pl.when(ci == n_chunks - 1)
    def _():
        sout_ref[0] = st_s[...]


def _gla_kernel(q_ref, k_ref, v_ref, r_ref, og_ref, wg_ref, bg_ref, gg_ref, s0_ref, bd_ref,
                tall_ref, masks_ref, o_ref, sout_ref, st_s, *, c, c_in, n_chunks):
    ci = pl.program_id(1)

    @pl.when(ci == 0)
    def _():
        st_s[...] = s0_ref[0]

    qs = _pad_rows(q_ref[...], c) * (GLA_K ** -0.5)
    kk = _pad_rows(k_ref[...], c)
    v = _pad_rows(v_ref[...], c)
    x = _dot(_pad_rows(r_ref[...], c).astype(BF16), wg_ref[...]) + bg_ref[...]
    g = (jnp.minimum(x, 0.0) - jnp.log(1.0 + jnp.exp(-jnp.abs(x)))) * (1.0 / GLA_GATE_NORM)
    if c_in < c:
        valid = lax.broadcasted_iota(jnp.int32, (c, 1), 0) < c_in
        g = jnp.where(valid, g, 0.0)
        kk = jnp.where(valid, kk, 0.0)

    levels = int(math.log2(c))
    e = _scan_exponents(g, tall_ref, c)
    eb = e[levels * c:(levels + 1) * c]
    eu = e[(levels + 1) * c:]
    st = st_s[...]
    vb = v.astype(BF16)
    klane = lax.broadcasted_iota(jnp.int32, (1, GLA_KW), 1) // GLA_K
    vlane = lax.broadcasted_iota(jnp.int32, (1, LIN_VW), 1) // LIN_V
    o = _dot_nt((qs * eb).astype(BF16), st.astype(BF16))
    for hd in range(N_HEADS_LIN):
        a = _scan_scores(qs, kk, e, masks_ref, c, qmask=klane == hd)
        o = o + jnp.where(vlane == hd, _dot(a.astype(BF16), vb), 0.0)
    upd = _dot(v.T.astype(BF16), (kk * eu).astype(BF16))
    st_s[...] = (st * eb[c - 1:c, :] + upd) * bd_ref[...]
    out = _gated_head_norm(o, _pad_rows(og_ref[...], c), gg_ref[...])
    o_ref[...] = out[:c_in]

    @pl.when(ci == n_chunks - 1)
    def _():
        sout_ref[0] = st_s[...]


def _scan_call(kernel, n_seq, n_chunks, c_in, tok_specs, const_specs, state_shape, args, name):
    rows = n_seq * n_chunks * c_in
    tok = [pl.BlockSpec((c_in, w), lambda n, ci, col=col: (n * n_chunks + ci, col))
           for (w, col) in tok_specs]
    const = [pl.BlockSpec(shape, lambda n, ci, nd=len(shape): (0,) * nd) for shape in const_specs]
    state = pl.BlockSpec((1,) + state_shape, lambda n, ci: (n,) + (0,) * len(state_shape))
    in_specs = tok + const[:-2] + [state] + const[-2:]
    return pl.pallas_call(
        kernel,
        grid=(n_seq, n_chunks),
        in_specs=in_specs,
        out_specs=[pl.BlockSpec((c_in, LIN_VW), lambda n, ci: (n * n_chunks + ci, 0)), state],
        out_shape=[jax.ShapeDtypeStruct((rows, LIN_VW), F32),
                   jax.ShapeDtypeStruct((n_seq,) + state_shape, F32)],
        scratch_shapes=[pltpu.VMEM(state_shape, F32)],
        compiler_params=pltpu.CompilerParams(
            dimension_semantics=("arbitrary", "arbitrary"),
            vmem_limit_bytes=_vmem_limit(48 << 20)),
        name=name,
    )(*args)


def _hgrn_scan(h, lb_logits, g_hgrn, s0, layer, n_seq, seq):
    c = SCAN_C
    c_in = min(seq, c)
    n_chunks = seq // c_in
    tall, masks = _scan_tables(c)
    s0t = jnp.swapaxes(s0, 2, 3)
    zero = jnp.zeros_like(s0t)
    even = jnp.concatenate([s0t, zero], axis=2)
    odd = jnp.concatenate([zero, s0t], axis=2)
    is_odd = (jnp.arange(N_HEADS_LIN) % 2 == 1)[None, :, None, None]
    s0k = jnp.where(is_odd, odd, even)
    tok_specs = [(HGRN_KW, 3), (HGRN_KW, 4), (LIN_VW, 10), (LIN_VW, 11)]
    const_specs = [(N_LAYERS, HGRN_KW), (1, LIN_VW), tall.shape, masks.shape]
    kernel = functools.partial(_hgrn_kernel, layer=layer, c=c, c_in=c_in, n_chunks=n_chunks)
    o, sk = _scan_call(kernel, n_seq, n_chunks, c_in, tok_specs, const_specs,
                       (N_HEADS_LIN, LANES, HGRN_K),
                       [h, h, h, h, lb_logits, g_hgrn, s0k, tall, masks], "hgrn_scan")
    sk = sk.reshape(n_seq, N_HEADS_LIN, 2, LIN_V, HGRN_K)
    s_even, s_odd = sk[:, :, 0], sk[:, :, 1]
    st = jnp.where(is_odd, s_odd, s_even)
    return o, jnp.swapaxes(st, 2, 3)


def _gla_scan(h, wg, bg, g_gla, s0, n_seq, seq):
    c = SCAN_C
    c_in = min(seq, c)
    n_chunks = seq // c_in
    tall, masks = _scan_tables(c)
    bd = (np.arange(LIN_VW)[:, None] // LIN_V == np.arange(GLA_KW)[None, :] // GLA_K)
    bd = jnp.asarray(bd.astype(np.float32))
    s0t = jnp.swapaxes(s0, 2, 3)
    s0k = jnp.zeros((n_seq, LIN_VW, GLA_KW), F32)
    for hd in range(N_HEADS_LIN):
        s0k = s0k.at[:, hd * LIN_V:(hd + 1) * LIN_V, hd * GLA_K:(hd + 1) * GLA_K].set(s0t[:, hd])
    tok_specs = [(GLA_KW, 24), (GLA_KW, 25), (LIN_VW, 13), (LANES, 30), (LIN_VW, 14)]
    rows = n_seq * n_chunks * c_in
    tok = [pl.BlockSpec((c_in, w), lambda n, ci, col=col: (n * n_chunks + ci, col))
           for (w, col) in tok_specs]
    cst = lambda shape: pl.BlockSpec(shape, lambda n, ci, nd=len(shape): (0,) * nd)
    state_shape = (LIN_VW, GLA_KW)
    state = pl.BlockSpec((1,) + state_shape, lambda n, ci: (n, 0, 0))
    kernel = functools.partial(_gla_kernel, c=c, c_in=c_in, n_chunks=n_chunks)
    o, sk = pl.pallas_call(
        kernel,
        grid=(n_seq, n_chunks),
        in_specs=tok + [cst((LANES, GLA_KW)), cst((1, GLA_KW)), cst((1, LIN_VW)), state,
                        cst(bd.shape), cst(tall.shape), cst(masks.shape)],
        out_specs=[pl.BlockSpec((c_in, LIN_VW), lambda n, ci: (n * n_chunks + ci, 0)), state],
        out_shape=[jax.ShapeDtypeStruct((rows, LIN_VW), F32),
                   jax.ShapeDtypeStruct((n_seq,) + state_shape, F32)],
        scratch_shapes=[pltpu.VMEM(state_shape, F32)],
        compiler_params=pltpu.CompilerParams(
            dimension_semantics=("arbitrary", "arbitrary"),
            vmem_limit_bytes=_vmem_limit(48 << 20)),
        name="gla_scan",
    )(h, h, h, h, h, wg, bg, g_gla, s0k, bd, tall, masks)
    st = jnp.stack([sk[:, hd * LIN_V:(hd + 1) * LIN_V, hd * GLA_K:(hd + 1) * GLA_K]
                    for hd in range(N_HEADS_LIN)], axis=1)
    return o, jnp.swapaxes(st, 2, 3)


def _gelu_tanh(x):
    return x * (0.5 * (1.0 + jnp.tanh(math.sqrt(2.0 / math.pi) * (x + 0.044715 * (x * x * x)))))


def _post_kernel(*refs, tm, seq_rows, tiles_per_seq):
    per_tile_seqs = seq_rows < tm
    if per_tile_seqs:
        (oa_ref, ob_ref, oc_ref, x_ref, wout_ref, gpm_ref, gpf_ref, wup_ref, wconv_ref,
         wdown_ref, gpo_ref, s1_ref, s2_ref, xo_ref, u_ref, ub_s) = refs
    else:
        (oa_ref, ob_ref, oc_ref, x_ref, wout_ref, gpm_ref, gpf_ref, wup_ref, wconv_ref,
         wdown_ref, gpo_ref, prev_ref, xo_ref, conv_ref, ub_s) = refs
    i = pl.program_id(0)
    w2 = 2 * D_FF
    mix = jnp.concatenate([oa_ref[...], ob_ref[...], oc_ref[...]], axis=1).astype(BF16)
    x1 = x_ref[...] + _rms(_dot(mix, wout_ref[...]), gpm_ref[...])
    xn = _rms(x1, gpf_ref[...]).astype(BF16)
    u = _dot(xn, wup_ref[...])
    h0 = SUBLANES
    if per_tile_seqs:
        ub_s[0:h0, :] = jnp.zeros((h0, w2), F32)
    else:
        @pl.when(i % tiles_per_seq == 0)
        def _():
            ub_s[0:h0 - 2, :] = jnp.zeros((h0 - 2, w2), F32)
            ub_s[h0 - 2:h0, :] = prev_ref[0]
    ub_s[h0:h0 + tm, :] = u
    sh1 = ub_s[h0 - 1:h0 - 1 + tm, :]
    sh2 = ub_s[h0 - 2:h0 - 2 + tm, :]
    if per_tile_seqs:
        t = lax.broadcasted_iota(jnp.int32, (tm, 1), 0) % seq_rows
        sh1 = jnp.where(t >= 1, sh1, s1_ref[...])
        sh2 = jnp.where(t >= 2, sh2, s2_ref[...])
        u_ref[...] = u
    else:
        conv_ref[0] = u[tm - 2:tm]
        ub_s[0:h0, :] = u[tm - h0:tm]
    wc = wconv_ref[...]
    cv = sh2 * wc[0:1] + sh1 * wc[1:2] + u * wc[2:3]
    act = (_gelu_tanh(cv[:, :D_FF]) * cv[:, D_FF:]).astype(BF16)
    ffn = _dot(act, wdown_ref[...])
    xo_ref[...] = x1 + _rms(ffn, gpo_ref[...])


def _post_mix_ffn(oa, ob, oc, x, wout, gpm, gpf, wup, wconv, wdown, gpo, conv_prev, n_seq, seq, tm):
    m = x.shape[0]
    w2 = 2 * D_FF
    per_tile_seqs = seq < tm
    tiles_per_seq = max(seq // tm, 1)
    row = lambda w: pl.BlockSpec((tm, w), lambda i: (i, 0))
    cst = lambda shape: pl.BlockSpec(shape, lambda i, nd=len(shape): (0,) * nd,
                                     pipeline_mode=pl.Buffered(1))
    in_specs = [row(ATT_W), row(LIN_VW), row(LIN_VW), row(D_MODEL),
                cst((D_MODEL, D_MODEL)), cst((1, D_MODEL)), cst((1, D_MODEL)),
                cst((D_MODEL, w2)), cst((CONV_W, w2)), cst((D_FF, D_MODEL)), cst((1, D_MODEL))]
    args = [oa, ob, oc, x, wout, gpm, gpf, wup, wconv, wdown, gpo]
    if per_tile_seqs:
        s1 = jnp.zeros((n_seq, seq, w2), F32).at[:, 0].set(conv_prev[:, 1])
        s2 = jnp.zeros((n_seq, seq, w2), F32).at[:, 0].set(conv_prev[:, 0]).at[:, 1].set(conv_prev[:, 1])
        in_specs += [row(w2), row(w2)]
        args += [s1.reshape(m, w2), s2.reshape(m, w2)]
        second = pl.BlockSpec((tm, w2), lambda i: (i, 0))
        second_shape = jax.ShapeDtypeStruct((m, w2), F32)
    else:
        in_specs += [pl.BlockSpec((1, CONV_W - 1, w2), lambda i: (i // tiles_per_seq, 0, 0))]
        args += [conv_prev]
        second = pl.BlockSpec((1, CONV_W - 1, w2), lambda i: (i // tiles_per_seq, 0, 0))
        second_shape = jax.ShapeDtypeStruct((n_seq, CONV_W - 1, w2), F32)
    est = ((D_MODEL * D_MODEL + D_MODEL * w2 + D_FF * D_MODEL) * 2
           + 6 * tm * w2 * 4 + 8 * tm * D_MODEL * 4 + (6 << 20))
    return pl.pallas_call(
        functools.partial(_post_kernel, tm=tm, seq_rows=seq, tiles_per_seq=tiles_per_seq),
        grid=(m // tm,),
        in_specs=in_specs,
        out_specs=[row(D_MODEL), second],
        out_shape=[jax.ShapeDtypeStruct((m, D_MODEL), F32), second_shape],
        scratch_shapes=[pltpu.VMEM((tm + SUBLANES, w2), F32)],
        compiler_params=pltpu.CompilerParams(
            dimension_semantics=("arbitrary",), vmem_limit_bytes=_vmem_limit(est)),
        name="post_mix_ffn",
    )(*args)


def _permute_w_in(w):
    r0 = IN_COLS - LIN_VW - GLA_RANK
    pad = jnp.zeros((w.shape[0], IN_COLS_PAD - IN_COLS), w.dtype)
    return jnp.concatenate([w[:, :r0], w[:, r0 + GLA_RANK:], w[:, r0:r0 + GLA_RANK], pad],
                           axis=1).astype(BF16)


def kernel(x_prompt, x_sample, cache_win_k, cache_win_v, state_hgrn, state_gla, state_conv,
           w_in, w_gate_up, b_gate_up, w_out, g_hgrn, g_gla, lb_logits,
           w_up, w_conv, w_down, g_pre_mix, g_post_mix, g_pre_ffn, g_post_ffn):
    nb, seq, d = x_prompt.shape
    db, t_new, _ = x_sample.shape
    win = cache_win_k.shape[2]
    tm = 256
    slopes = jnp.asarray(2.0 ** (-8.0 * np.arange(1, N_HEADS_ATT + 1) / N_HEADS_ATT), F32)
    ck = cache_win_k.reshape(N_LAYERS, db, win, ATT_W)
    cv = cache_win_v.reshape(N_LAYERS, db, win, ATT_W)

    xp = x_prompt.reshape(nb * seq, d)
    xs = x_sample.reshape(db * t_new, d)
    zero_hgrn = jnp.zeros((nb, N_HEADS_LIN, HGRN_K, LIN_V), F32)
    zero_gla = jnp.zeros((nb, N_HEADS_LIN, GLA_K, LIN_V), F32)
    zero_conv = jnp.zeros((nb, CONV_W - 1, 2 * D_FF), F32)
    keep = min(WIN_MAX, seq)

    wk_buf = wv_buf = None
    wkp, wvp, hg_p, hg_s, gl_p, gl_s, cv_p, cv_s = [], [], [], [], [], [], [], []
    for l in range(N_LAYERS):
        w_in_l = _permute_w_in(w_in[l])
        wg = jnp.zeros((LANES, GLA_KW), F32).at[:GLA_RANK].set(w_gate_up[l]).astype(BF16)
        bg = b_gate_up[l].reshape(1, GLA_KW)
        gpre = g_pre_mix[l].reshape(1, d)
        gh = g_hgrn[l].reshape(1, LIN_VW)
        gg = g_gla[l].reshape(1, LIN_VW)
        post_w = (w_out[l].astype(BF16), g_post_mix[l].reshape(1, d), g_pre_ffn[l].reshape(1, d),
                  w_up[l].astype(BF16), w_conv[l], w_down[l].astype(BF16), g_post_ffn[l].reshape(1, d))

        hp = _norm_matmul(xp, gpre, w_in_l, tm)
        oa = _attn_prompt(hp, slopes, nb, seq)
        ob, sh = _hgrn_scan(hp, lb_logits, gh, zero_hgrn, l, nb, seq)
        oc, sg = _gla_scan(hp, wg, bg, gg, zero_gla, nb, seq)
        xp, conv_p = _post_mix_ffn(oa, ob, oc, xp, *post_w, zero_conv, nb, seq, tm)
        h3 = hp.reshape(nb, seq, IN_COLS_PAD)
        wkp.append(h3[:, seq - keep:, ATT_W:2 * ATT_W])
        wvp.append(h3[:, seq - keep:, 2 * ATT_W:3 * ATT_W])
        hg_p.append(sh)
        gl_p.append(sg)
        cv_p.append(conv_p)

        hs = _norm_matmul(xs, gpre, w_in_l, db * t_new)
        oa, wk_buf, wv_buf = _attn_decode(hs, ck, cv, wk_buf, wv_buf, l, db, t_new, win)
        ob, sh = _hgrn_scan(hs, lb_logits, gh, state_hgrn[l], l, db, t_new)
        oc, sg = _gla_scan(hs, wg, bg, gg, state_gla[l], db, t_new)
        xs, u_s = _post_mix_ffn(oa, ob, oc, xs, *post_w, state_conv[l], db, t_new, db * t_new)
        hg_s.append(sh)
        gl_s.append(sg)
        cv_s.append(u_s.reshape(db, t_new, 2 * D_FF)[:, t_new - (CONV_W - 1):])

    hshape = (N_LAYERS, nb, keep, N_HEADS_ATT, HEAD_DIM)
    sshape = (N_LAYERS, db, win, N_HEADS_ATT, HEAD_DIM)
    return (xp.reshape(nb, seq, d), xs.reshape(db, t_new, d),
            jnp.stack(wkp).reshape(hshape), jnp.stack(wvp).reshape(hshape),
            wk_buf.reshape(sshape), wv_buf.reshape(sshape),
            jnp.stack(hg_p), jnp.stack(hg_s), jnp.stack(gl_p), jnp.stack(gl_s),
            jnp.stack(cv_p), jnp.stack(cv_s))
```

```python
import functools
import math

import numpy as np
import jax
import jax.numpy as jnp
from jax import lax
from jax.experimental import pallas as pl
from jax.experimental.pallas import tpu as pltpu

F32 = jnp.float32
BF16 = jnp.bfloat16

D_MODEL = 1024
N_LAYERS = 4
HEAD_DIM = 64
N_HEADS_ATT = 8
ATT_W = N_HEADS_ATT * HEAD_DIM
ATT_J = 128
ATT_DILATIONS = (1, 4, 16)
WIN_MAX = 2048
ATT_UNROLL = 4
N_HEADS_LIN = 4
HGRN_K = 128
GLA_K = 32
LIN_V = 64
LIN_VW = N_HEADS_LIN * LIN_V
HGRN_KW = N_HEADS_LIN * HGRN_K
GLA_KW = N_HEADS_LIN * GLA_K
GLA_RANK = 16
GLA_GATE_NORM = 16.0
D_FF = 2816
CONV_W = 3
RMS_EPS = 1e-6
IN_COLS = 3856
IN_COLS_PAD = 3968

LANES = 128
SUBLANES = 8
VMEM_LIMIT_CAP = 56 * 1024 * 1024

SCAN_C = 128
NEG_BIG = 1e30


def _vmem_limit(nbytes):
    return int(min(VMEM_LIMIT_CAP, max(32 * 1024 * 1024, nbytes)))


def _rms(x, g):
    ms = jnp.mean(x * x, axis=-1, keepdims=True)
    return x * lax.rsqrt(ms + RMS_EPS) * g


def _dot(a, b):
    return jnp.dot(a, b, preferred_element_type=F32)


def _dot_nt(a, b):
    return lax.dot_general(a, b, (((1,), (1,)), ((), ())), preferred_element_type=F32)


def _norm_matmul_kernel(x_ref, g_ref, w_ref, o_ref):
    xn = _rms(x_ref[...], g_ref[...]).astype(BF16)
    o_ref[...] = _dot(xn, w_ref[...])


def _norm_matmul(x, g, w, tm):
    m, d = x.shape
    n = w.shape[1]
    est = 2 * tm * d * 4 + 2 * d * n * 2 + 2 * tm * n * 4 + (4 << 20)
    return pl.pallas_call(
        _norm_matmul_kernel,
        grid=(m // tm,),
        in_specs=[
            pl.BlockSpec((tm, d), lambda i: (i, 0)),
            pl.BlockSpec((1, d), lambda i: (0, 0)),
            pl.BlockSpec((d, n), lambda i: (0, 0)),
        ],
        out_specs=pl.BlockSpec((tm, n), lambda i: (i, 0)),
        out_shape=jax.ShapeDtypeStruct((m, n), F32),
        compiler_params=pltpu.CompilerParams(
            dimension_semantics=("arbitrary",), vmem_limit_bytes=_vmem_limit(est)),
        name="norm_matmul",
    )(x, g, w)


def _attn_prompt_kernel(slopes_ref, q_ref, k_ref, v_ref, o_ref, m_s, l_s, acc_s, bias_s, *, seq):
    hp = pl.program_id(1)
    j = ATT_J
    qi = lax.broadcasted_iota(jnp.int32, (j, 2 * j), 0)
    ki = lax.broadcasted_iota(jnp.int32, (j, 2 * j), 1)
    d1 = j + qi - ki
    dt1 = jnp.where((d1 >= 0) & (d1 <= j), d1.astype(F32), NEG_BIG)
    d0 = qi - ki
    dt0 = jnp.where((ki < j) & (d0 >= 0), d0.astype(F32), NEG_BIG)
    for g, dil in enumerate(ATT_DILATIONS):
        for half in (0, 1):
            sc = slopes_ref[2 * hp + half] * float(dil)
            bias_s[g, half, 0] = -sc * dt0
            bias_s[g, half, 1] = -sc * dt1
    lane = lax.broadcasted_iota(jnp.int32, (1, LANES), 1)
    half1 = lane >= HEAD_DIM
    ones = jnp.ones((2 * j, LANES), BF16)

    def branch(g, dil, init):
        nb = seq // (j * dil)
        lognb = int(math.log2(nb))
        unroll = ATT_UNROLL
        assert unroll % nb == 0 or nb % unroll == 0

        def rows(start, size):
            return pl.ds(start, size) if dil == 1 else pl.ds(start, size, stride=dil)

        def load(idx, u):
            b = idx & (nb - 1)
            r = idx >> lognb
            start = b * (j * dil) + r
            if unroll % nb == 0:
                first = u % nb == 0
                kstart = r if first else start - j * dil
                sel = 0 if first else 1
            elif u == 0:
                first = b == 0
                kstart = jnp.where(first, r, start - j * dil)
                sel = jnp.where(first, 0, 1)
            else:
                kstart = start - j * dil
                sel = 1
            if dil == 1:
                start = pl.multiple_of(start, j)
                kstart = pl.multiple_of(kstart, j)
            q = q_ref[rows(start, j), :] * (HEAD_DIM ** -0.5)
            kk = k_ref[rows(kstart, 2 * j), :].astype(BF16)
            vv = v_ref[rows(kstart, 2 * j), :].astype(BF16)
            old = None
            if not init:
                old = (m_s[0, rows(start, j), :], m_s[1, rows(start, j), :],
                       l_s[0, rows(start, j), :], l_s[1, rows(start, j), :],
                       acc_s[rows(start, j), :])
            return start, sel, q, kk, vv, old

        def compute(sel, q, kk, vv, old):
            vx = jnp.concatenate([vv, ones], axis=1)
            m_out, l_out, acc_new = [], [], None
            for half in (0, 1):
                hmask = half1 if half else jnp.logical_not(half1)
                qh = jnp.where(hmask, q, 0.0).astype(BF16)
                s = _dot_nt(qh, kk) + bias_s[g, half, sel]
                mx = jnp.max(s, axis=-1, keepdims=True)
                if init:
                    m_new = jnp.broadcast_to(mx, (j, LANES))
                else:
                    m_new = jnp.maximum(old[half], mx)
                p = jnp.exp(s - jnp.concatenate([m_new, m_new], axis=1)).astype(BF16)
                pvx = _dot(p, vx)
                pv, rs = pvx[:, :LANES], pvx[:, LANES:]
                if init:
                    l_new, contrib = rs, pv
                else:
                    alpha = jnp.exp(old[half] - m_new)
                    l_new = alpha * old[2 + half] + rs
                    contrib = alpha * old[4] + pv
                m_out.append(m_new)
                l_out.append(l_new)
                acc_new = contrib if half == 0 else jnp.where(half1, contrib, acc_new)
            return m_out, l_out, acc_new

        def body(it, carry):
            loaded = [load(it * unroll + u, u) for u in range(unroll)]
            done = [(ld[0],) + compute(*ld[1:]) for ld in loaded]
            for start, m_out, l_out, acc_new in done:
                for half in (0, 1):
                    m_s[half, rows(start, j), :] = m_out[half]
                    l_s[half, rows(start, j), :] = l_out[half]
                acc_s[rows(start, j), :] = acc_new
            return carry

        lax.fori_loop(0, nb * dil // unroll, body, 0)

    order = sorted(range(len(ATT_DILATIONS)), key=lambda g: -ATT_DILATIONS[g])
    for n, g in enumerate(order):
        branch(g, ATT_DILATIONS[g], n == 0)

    def finish(c, carry):
        rows = pl.ds(pl.multiple_of(c * j, j), j)
        l = jnp.where(half1, l_s[1, rows, :], l_s[0, rows, :])
        o_ref[rows, :] = acc_s[rows, :] / l
        return carry

    lax.fori_loop(0, seq // j, finish, 0)


def _attn_prompt(h, slopes, n_seq, seq):
    npairs = ATT_W // LANES
    blk = (seq, LANES)
    est = 5 * 2 * seq * LANES * 4 + 5 * seq * LANES * 4 + (8 << 20)
    return pl.pallas_call(
        functools.partial(_attn_prompt_kernel, seq=seq),
        grid=(n_seq, npairs),
        in_specs=[
            pl.BlockSpec(memory_space=pltpu.SMEM),
            pl.BlockSpec(blk, lambda n, p: (n, p)),
            pl.BlockSpec(blk, lambda n, p: (n, npairs + p)),
            pl.BlockSpec(blk, lambda n, p: (n, 2 * npairs + p)),
        ],
        out_specs=pl.BlockSpec(blk, lambda n, p: (n, p)),
        out_shape=jax.ShapeDtypeStruct((n_seq * seq, ATT_W), F32),
        scratch_shapes=[
            pltpu.VMEM((2, seq, LANES), F32),
            pltpu.VMEM((2, seq, LANES), F32),
            pltpu.VMEM((seq, LANES), F32),
            pltpu.VMEM((len(ATT_DILATIONS), 2, 2, ATT_J, 2 * ATT_J), F32),
        ],
        compiler_params=pltpu.CompilerParams(
            dimension_semantics=("arbitrary", "arbitrary"), vmem_limit_bytes=_vmem_limit(est)),
        name="attn_prompt",
    )(slopes, h, h, h)


def _attn_decode_tables(t_new, win):
    cols = win + LANES
    c = np.arange(cols)[None, :]
    i = np.arange(t_new)[:, None]
    delta = win + i - c
    ok = (delta >= 0) & (c < win + t_new)
    mult = np.zeros((t_new, cols), np.float32)
    for dil in ATT_DILATIONS:
        mult += (ok & (delta % dil == 0) & (delta <= ATT_J * dil)).astype(np.float32)
    slopes = 2.0 ** (-8.0 * np.arange(1, N_HEADS_ATT + 1) / N_HEADS_ATT)
    bias = -slopes[:, None, None] * delta[None].astype(np.float64)
    bias = np.where(mult[None] > 0, bias, -NEG_BIG).astype(np.float32)
    bias = bias.reshape(N_HEADS_ATT * t_new, cols)
    mult = np.tile(mult, (N_HEADS_ATT, 1))
    return bias, mult


def _attn_decode_kernel(bias_ref, mult_ref, q_ref, k_ref, v_ref, ck_ref, cv_ref,
                        o_ref, wk_ref, wv_ref, *, t_new, win):
    lane = lax.broadcasted_iota(jnp.int32, (1, ATT_W), 1)
    q = q_ref[...] * (HEAD_DIM ** -0.5)
    k_new = k_ref[...]
    v_new = v_ref[...]
    qs = jnp.concatenate(
        [jnp.where((lane // HEAD_DIM) == h, q, 0.0) for h in range(N_HEADS_ATT)], axis=0
    ).astype(BF16)
    pad = jnp.zeros((LANES - t_new, ATT_W), F32)
    k_tail = jnp.concatenate([k_new, pad], axis=0).astype(BF16)
    v_tail = jnp.concatenate([v_new, pad], axis=0).astype(BF16)
    ck = ck_ref[...]
    cv = cv_ref[...]
    s = jnp.concatenate([_dot_nt(qs, ck.astype(BF16)), _dot_nt(qs, k_tail)], axis=1)
    s = s + bias_ref[...]
    mx = jnp.max(s, axis=-1, keepdims=True)
    p = mult_ref[...] * jnp.exp(s - mx)
    l = jnp.sum(p, axis=-1, keepdims=True)
    pb = p.astype(BF16)
    o_all = (_dot(pb[:, :win], cv.astype(BF16)) + _dot(pb[:, win:], v_tail)) / l
    o = jnp.zeros((t_new, ATT_W), F32)
    for h in range(N_HEADS_ATT):
        o = jnp.where((lane // HEAD_DIM) == h, o_all[h * t_new:(h + 1) * t_new], o)
    o_ref[...] = o
    wk_ref[0:win - t_new, :] = ck[t_new:win]
    wk_ref[win - t_new:win, :] = k_new
    wv_ref[0:win - t_new, :] = cv[t_new:win]
    wv_ref[win - t_new:win, :] = v_new


def _attn_decode(h, cache_k, cache_v, wk_buf, wv_buf, layer, n_seq, t_new, win):
    bias, mult = _attn_decode_tables(t_new, win)
    rows = N_HEADS_ATT * t_new
    cols = win + LANES
    tab = pl.BlockSpec((rows, cols), lambda n: (0, 0))
    tok = lambda c: pl.BlockSpec((t_new, ATT_W), lambda n, c=c: (n, c))
    cache = pl.BlockSpec((None, None, win, ATT_W), lambda n: (layer, n, 0, 0))
    in_specs = [tab, tab, tok(0), tok(1), tok(2), cache, cache]
    args = [jnp.asarray(bias), jnp.asarray(mult), h, h, h, cache_k, cache_v]
    aliases = {}
    if wk_buf is not None:
        in_specs += [pl.BlockSpec(memory_space=pl.ANY), pl.BlockSpec(memory_space=pl.ANY)]
        args += [wk_buf, wv_buf]
        aliases = {7: 1, 8: 2}
    kernel = functools.partial(_attn_decode_kernel, t_new=t_new, win=win)
    if wk_buf is not None:
        kernel = lambda *refs, _k=kernel: _k(*refs[:7], *refs[9:])
    win_shape = jax.ShapeDtypeStruct((N_LAYERS, n_seq, win, ATT_W), F32)
    est = 8 * win * ATT_W * 4 + 4 * win * ATT_W * 4 + (8 << 20)
    return pl.pallas_call(
        kernel,
        grid=(n_seq,),
        in_specs=in_specs,
        out_specs=[pl.BlockSpec((t_new, ATT_W), lambda n: (n, 0)), cache, cache],
        out_shape=[jax.ShapeDtypeStruct((n_seq * t_new, ATT_W), F32), win_shape, win_shape],
        input_output_aliases=aliases,
        compiler_params=pltpu.CompilerParams(
            dimension_semantics=("arbitrary",), vmem_limit_bytes=_vmem_limit(est)),
        name="attn_decode",
    )(*args)


SCAN_MXU_LEVELS = 3


def _scan_tables(c):
    levels = int(math.log2(c))
    t = np.arange(c)[:, None]
    u = np.arange(c)[None, :]
    mats, masks = [], [np.eye(c, dtype=np.float32)]
    for li in range(levels):
        m = 1 << li
        bnd = (t // (2 * m)) * 2 * m + m - 1
        right = (t % (2 * m)) >= m
        if li < SCAN_MXU_LEVELS:
            mats.append(np.where(right, (u > bnd) & (u <= t), (u > t) & (u <= bnd)))
        masks.append(((t // (2 * m)) == (u // (2 * m))) & right & ((u % (2 * m)) < m))
    mats.append(u <= t)
    tall = np.concatenate(mats, axis=0).astype(np.float32)
    return jnp.asarray(tall, BF16), jnp.asarray(np.stack(masks).astype(np.float32))


def _scan_exponents(g, tall_ref, c):
    levels = int(math.log2(c))
    g1 = g.astype(BF16)
    r1 = g - g1.astype(F32)
    g2 = r1.astype(BF16)
    g3 = (r1 - g2.astype(F32)).astype(BF16)
    arg = _dot(tall_ref[...], jnp.concatenate([g1, g2, g3], axis=1))
    arg = arg[:, :LANES] + arg[:, LANES:2 * LANES] + arg[:, 2 * LANES:]
    small = jnp.exp(arg[:SCAN_MXU_LEVELS * c])
    e = [small[li * c:(li + 1) * c] for li in range(SCAN_MXU_LEVELS)]
    b = arg[SCAN_MXU_LEVELS * c:]
    row = lax.broadcasted_iota(jnp.int32, (c, 1), 0)
    for li in range(SCAN_MXU_LEVELS, levels):
        m = 1 << li
        bnd = jnp.concatenate(
            [jnp.broadcast_to(b[blk * 2 * m + m - 1:blk * 2 * m + m], (2 * m, LANES))
             for blk in range(c // (2 * m))], axis=0)
        sign = jnp.where((row & (2 * m - 1)) >= m, 1.0, -1.0)
        e.append(jnp.exp((b - bnd) * sign))
    return e, jnp.exp(b), jnp.exp(b[c - 1:c] - b)


def _scan_scores(qs, kk, e, masks_ref, c, qmask=None):
    levels = int(math.log2(c))
    row = lax.broadcasted_iota(jnp.int32, (c, 1), 0)

    def sel(x):
        return x if qmask is None else jnp.where(qmask, x, 0.0)

    a = masks_ref[0] * _dot_nt(sel(qs).astype(BF16), kk.astype(BF16))
    for li in range(levels):
        m = 1 << li
        right = (row & (2 * m - 1)) >= m
        w = jnp.where(right, qs, kk) * e[li]
        a = a + masks_ref[li + 1] * _dot_nt(sel(w).astype(BF16), w.astype(BF16))
    return a


def _pad_rows(x, c):
    if x.shape[0] == c:
        return x
    return jnp.concatenate([x, jnp.zeros((c - x.shape[0], x.shape[1]), x.dtype)], axis=0)


def _gated_head_norm(o, gate, g):
    lane = lax.broadcasted_iota(jnp.int32, (1, LANES), 1)
    half1 = lane >= LIN_V
    o2 = o * o
    parts = []
    for p in range(LIN_VW // LANES):
        o2p = o2[:, p * LANES:(p + 1) * LANES]
        s_all = jnp.sum(o2p, axis=-1, keepdims=True)
        s_hi = jnp.sum(jnp.where(half1, o2p, 0.0), axis=-1, keepdims=True)
        parts.append(jnp.where(half1, s_hi, s_all - s_hi))
    ms = jnp.concatenate(parts, axis=1) * (1.0 / LIN_V)
    on = o * lax.rsqrt(ms + RMS_EPS)
    return on * g * (gate * (1.0 / (1.0 + jnp.exp(-gate))))


def _hgrn_kernel(q_ref, f_ref, i_ref, og_ref, lbl_ref, gh_ref, s0_ref, tall_ref, masks_ref,
                 o_ref, sout_ref, st_s, *, layer, c, c_in, n_chunks):
    ci = pl.program_id(1)

    @pl.when(ci == 0)
    def _():
        st_s[...] = s0_ref[0]

    lbl = lbl_ref[...]
    ex = jnp.exp(lbl - jnp.max(lbl, axis=0, keepdims=True))
    prob = ex / jnp.sum(ex, axis=0, keepdims=True)
    lb = jnp.zeros((1, HGRN_KW), F32)
    for r in range(1, layer + 1):
        lb = lb + prob[r:r + 1]

    q = _pad_rows(q_ref[...], c)
    f = _pad_rows(f_ref[...], c)
    vi = _pad_rows(i_ref[...], c)
    forget = lb + (1.0 - lb) * (1.0 / (1.0 + jnp.exp(-f)))
    g = jnp.log(forget)
    k = 1.0 - forget
    if c_in < c:
        valid = lax.broadcasted_iota(jnp.int32, (c, 1), 0) < c_in
        g = jnp.where(valid, g, 0.0)
        k = jnp.where(valid, k, 0.0)

    lane = lax.broadcasted_iota(jnp.int32, (1, LANES), 1)
    half1 = lane >= LIN_V
    pairs = []
    for p in range(N_HEADS_LIN // 2):
        v = vi[:, p * LANES:(p + 1) * LANES]
        vb = v.astype(BF16)
        vt = v.T.astype(BF16)
        o_pair = None
        for half in range(2):
            hd = 2 * p + half
            cols = slice(hd * HGRN_K, (hd + 1) * HGRN_K)
            qs = q[:, cols] * (HGRN_K ** -0.5)
            kk = k[:, cols]
            e, eb, eu = _scan_exponents(g[:, cols], tall_ref, c)
            a = _scan_scores(qs, kk, e, masks_ref, c)
            st = st_s[hd]
            o_h = _dot(a.astype(BF16), vb) + _dot_nt((qs * eb).astype(BF16), st.astype(BF16))
            st_s[hd] = st * eb[c - 1:c, :] + _dot(vt, (kk * eu).astype(BF16))
            o_pair = o_h if half == 0 else jnp.where(half1, o_h, o_pair)
        pairs.append(o_pair)
    o = jnp.concatenate(pairs, axis=1)
    out = _gated_head_norm(o, _pad_rows(og_ref[...], c), gh_ref[...])
    o_ref[...] = out[:c_in]

    @pl.when(ci == n_chunks - 1)
    def _():
        sout_ref[0] = st_s[...]


def _gla_kernel(q_ref, k_ref, v_ref, r_ref, og_ref, wg_ref, bg_ref, gg_ref, s0_ref, bd_ref,
                tall_ref, masks_ref, o_ref, sout_ref, st_s, *, c, c_in, n_chunks):
    ci = pl.program_id(1)

    @pl.when(ci == 0)
    def _():
        st_s[...] = s0_ref[0]

    qs = _pad_rows(q_ref[...], c) * (GLA_K ** -0.5)
    kk = _pad_rows(k_ref[...], c)
    v = _pad_rows(v_ref[...], c)
    x = _dot(_pad_rows(r_ref[...], c).astype(BF16), wg_ref[...]) + bg_ref[...]
    g = (jnp.minimum(x, 0.0) - jnp.log(1.0 + jnp.exp(-jnp.abs(x)))) * (1.0 / GLA_GATE_NORM)
    if c_in < c:
        valid = lax.broadcasted_iota(jnp.int32, (c, 1), 0) < c_in
        g = jnp.where(valid, g, 0.0)
        kk = jnp.where(valid, kk, 0.0)

    e, eb, eu = _scan_exponents(g, tall_ref, c)
    st = st_s[...]
    vb = v.astype(BF16)
    klane = lax.broadcasted_iota(jnp.int32, (1, GLA_KW), 1) // GLA_K
    vlane = lax.broadcasted_iota(jnp.int32, (1, LIN_VW), 1) // LIN_V
    o = _dot_nt((qs * eb).astype(BF16), st.astype(BF16))
    for hd in range(N_HEADS_LIN):
        a = _scan_scores(qs, kk, e, masks_ref, c, qmask=klane == hd)
        o = o + jnp.where(vlane == hd, _dot(a.astype(BF16), vb), 0.0)
    upd = _dot(v.T.astype(BF16), (kk * eu).astype(BF16))
    st_s[...] = (st * eb[c - 1:c, :] + upd) * bd_ref[...]
    out = _gated_head_norm(o, _pad_rows(og_ref[...], c), gg_ref[...])
    o_ref[...] = out[:c_in]

    @pl.when(ci == n_chunks - 1)
    def _():
        sout_ref[0] = st_s[...]


def _scan_call(kernel, n_seq, n_chunks, c_in, tok_specs, const_specs, state_shape, args, name):
    rows = n_seq * n_chunks * c_in
    tok = [pl.BlockSpec((c_in, w), lambda n, ci, col=col: (n * n_chunks + ci, col))
           for (w, col) in tok_specs]
    const = [pl.BlockSpec(shape, lambda n, ci, nd=len(shape): (0,) * nd) for shape in const_specs]
    state = pl.BlockSpec((1,) + state_shape, lambda n, ci: (n,) + (0,) * len(state_shape))
    in_specs = tok + const[:-2] + [state] + const[-2:]
    return pl.pallas_call(
        kernel,
        grid=(n_seq, n_chunks),
        in_specs=in_specs,
        out_specs=[pl.BlockSpec((c_in, LIN_VW), lambda n, ci: (n * n_chunks + ci, 0)), state],
        out_shape=[jax.ShapeDtypeStruct((rows, LIN_VW), F32),
                   jax.ShapeDtypeStruct((n_seq,) + state_shape, F32)],
        scratch_shapes=[pltpu.VMEM(state_shape, F32)],
        compiler_params=pltpu.CompilerParams(
            dimension_semantics=("arbitrary", "arbitrary"),
            vmem_limit_bytes=_vmem_limit(48 << 20)),
        name=name,
    )(*args)


def _hgrn_scan(h, lb_logits, g_hgrn, s0, layer, n_seq, seq):
    c = SCAN_C
    c_in = min(seq, c)
    n_chunks = seq // c_in
    tall, masks = _scan_tables(c)
    s0t = jnp.swapaxes(s0, 2, 3)
    zero = jnp.zeros_like(s0t)
    even = jnp.concatenate([s0t, zero], axis=2)
    odd = jnp.concatenate([zero, s0t], axis=2)
    is_odd = (jnp.arange(N_HEADS_LIN) % 2 == 1)[None, :, None, None]
    s0k = jnp.where(is_odd, odd, even)
    tok_specs = [(HGRN_KW, 3), (HGRN_KW, 4), (LIN_VW, 10), (LIN_VW, 11)]
    const_specs = [(N_LAYERS, HGRN_KW), (1, LIN_VW), tall.shape, masks.shape]
    kernel = functools.partial(_hgrn_kernel, layer=layer, c=c, c_in=c_in, n_chunks=n_chunks)
    o, sk = _scan_call(kernel, n_seq, n_chunks, c_in, tok_specs, const_specs,
                       (N_HEADS_LIN, LANES, HGRN_K),
                       [h, h, h, h, lb_logits, g_hgrn, s0k, tall, masks], "hgrn_scan")
    sk = sk.reshape(n_seq, N_HEADS_LIN, 2, LIN_V, HGRN_K)
    s_even, s_odd = sk[:, :, 0], sk[:, :, 1]
    st = jnp.where(is_odd, s_odd, s_even)
    return o, jnp.swapaxes(st, 2, 3)


def _gla_scan(h, wg, bg, g_gla, s0, n_seq, seq):
    c = SCAN_C
    c_in = min(seq, c)
    n_chunks = seq // c_in
    tall, masks = _scan_tables(c)
    bd = (np.arange(LIN_VW)[:, None] // LIN_V == np.arange(GLA_KW)[None, :] // GLA_K)
    bd = jnp.asarray(bd.astype(np.float32))
    s0t = jnp.swapaxes(s0, 2, 3)
    eye = jnp.eye(N_HEADS_LIN, dtype=F32)[None, :, None, :, None]
    s0k = (s0t[:, :, :, None, :] * eye).reshape(n_seq, LIN_VW, GLA_KW)
    tok_specs = [(GLA_KW, 24), (GLA_KW, 25), (LIN_VW, 13), (LANES, 30), (LIN_VW, 14)]
    rows = n_seq * n_chunks * c_in
    tok = [pl.BlockSpec((c_in, w), lambda n, ci, col=col: (n * n_chunks + ci, col))
           for (w, col) in tok_specs]
    cst = lambda shape: pl.BlockSpec(shape, lambda n, ci, nd=len(shape): (0,) * nd)
    state_shape = (LIN_VW, GLA_KW)
    state = pl.BlockSpec((1,) + state_shape, lambda n, ci: (n, 0, 0))
    kernel = functools.partial(_gla_kernel, c=c, c_in=c_in, n_chunks=n_chunks)
    o, sk = pl.pallas_call(
        kernel,
        grid=(n_seq, n_chunks),
        in_specs=tok + [cst((LANES, GLA_KW)), cst((1, GLA_KW)), cst((1, LIN_VW)), state,
                        cst(bd.shape), cst(tall.shape), cst(masks.shape)],
        out_specs=[pl.BlockSpec((c_in, LIN_VW), lambda n, ci: (n * n_chunks + ci, 0)), state],
        out_shape=[jax.ShapeDtypeStruct((rows, LIN_VW), F32),
                   jax.ShapeDtypeStruct((n_seq,) + state_shape, F32)],
        scratch_shapes=[pltpu.VMEM(state_shape, F32)],
        compiler_params=pltpu.CompilerParams(
            dimension_semantics=("arbitrary", "arbitrary"),
            vmem_limit_bytes=_vmem_limit(48 << 20)),
        name="gla_scan",
    )(h, h, h, h, h, wg, bg, g_gla, s0k, bd, tall, masks)
    st = jnp.stack([sk[:, hd * LIN_V:(hd + 1) * LIN_V, hd * GLA_K:(hd + 1) * GLA_K]
                    for hd in range(N_HEADS_LIN)], axis=1)
    return o, jnp.swapaxes(st, 2, 3)


def _gelu_tanh(x):
    return x * (0.5 * (1.0 + jnp.tanh(math.sqrt(2.0 / math.pi) * (x + 0.044715 * (x * x * x)))))


FFN_CHUNK = 256
FFN_SLABS = 8


def _post_kernel(*refs, tm, seq_rows, tiles_per_seq):
    per_tile_seqs = seq_rows < tm
    if per_tile_seqs:
        (oa_ref, ob_ref, oc_ref, x_ref, wout_ref, gpm_ref, gpf_ref, wup_ref, wconv_ref,
         wdown_ref, gpo_ref, s1_ref, s2_ref, xo_ref, u_ref, slab_s, halo_s) = refs
    else:
        (oa_ref, ob_ref, oc_ref, x_ref, wout_ref, gpm_ref, gpf_ref, wup_ref, wconv_ref,
         wdown_ref, gpo_ref, prev_ref, xo_ref, conv_ref, slab_s, halo_s) = refs
    i = pl.program_id(0)
    w2 = 2 * D_FF
    h0 = SUBLANES
    mix = jnp.concatenate([oa_ref[...], ob_ref[...], oc_ref[...]], axis=1).astype(BF16)
    x1 = x_ref[...] + _rms(_dot(mix, wout_ref[...]), gpm_ref[...])
    xn = _rms(x1, gpf_ref[...]).astype(BF16)
    if per_tile_seqs:
        halo_s[...] = jnp.zeros((h0, w2), F32)
        t = lax.broadcasted_iota(jnp.int32, (tm, 1), 0) % seq_rows
    else:
        @pl.when(i % tiles_per_seq == 0)
        def _():
            halo_s[0:h0 - 2, :] = jnp.zeros((h0 - 2, w2), F32)
            halo_s[h0 - 2:h0, :] = prev_ref[0]

    def conv_tile(u_tile, col, ti):
        cols = slice(col, col + LANES)
        slab_s[ti, 0:h0, :] = halo_s[:, cols]
        slab_s[ti, h0:h0 + tm, :] = u_tile
        sh1 = slab_s[ti, h0 - 1:h0 - 1 + tm, :]
        sh2 = slab_s[ti, h0 - 2:h0 - 2 + tm, :]
        if per_tile_seqs:
            sh1 = jnp.where(t >= 1, sh1, s1_ref[:, cols])
            sh2 = jnp.where(t >= 2, sh2, s2_ref[:, cols])
            u_ref[:, cols] = u_tile
        else:
            conv_ref[0, :, cols] = u_tile[tm - 2:tm]
            halo_s[:, cols] = u_tile[tm - h0:tm]
        return sh2 * wconv_ref[0:1, cols] + sh1 * wconv_ref[1:2, cols] + u_tile * wconv_ref[2:3, cols]

    ffn = jnp.zeros((tm, D_MODEL), F32)
    tiles = FFN_CHUNK // LANES
    for ci, c0 in enumerate(range(0, D_FF, FFN_CHUNK)):
        parts = []
        for pi, base in enumerate((c0, D_FF + c0)):
            u = _dot(xn, wup_ref[:, base:base + FFN_CHUNK])
            slab0 = ((ci % 2) * 2 + pi) * tiles
            parts.append(jnp.concatenate(
                [conv_tile(u[:, k * LANES:(k + 1) * LANES], base + k * LANES, slab0 + k)
                 for k in range(tiles)], axis=1))
        act = (_gelu_tanh(parts[0]) * parts[1]).astype(BF16)
        ffn = ffn + _dot(act, wdown_ref[c0:c0 + FFN_CHUNK, :])
    xo_ref[...] = x1 + _rms(ffn, gpo_ref[...])


def _post_mix_ffn(oa, ob, oc, x, wout, gpm, gpf, wup, wconv, wdown, gpo, conv_prev, n_seq, seq, tm):
    m = x.shape[0]
    w2 = 2 * D_FF
    per_tile_seqs = seq < tm
    tiles_per_seq = max(seq // tm, 1)
    row = lambda w: pl.BlockSpec((tm, w), lambda i: (i, 0))
    cst = lambda shape: pl.BlockSpec(shape, lambda i, nd=len(shape): (0,) * nd,
                                     pipeline_mode=pl.Buffered(1))
    in_specs = [row(ATT_W), row(LIN_VW), row(LIN_VW), row(D_MODEL),
                cst((D_MODEL, D_MODEL)), cst((1, D_MODEL)), cst((1, D_MODEL)),
                cst((D_MODEL, w2)), cst((CONV_W, w2)), cst((D_FF, D_MODEL)), cst((1, D_MODEL))]
    args = [oa, ob, oc, x, wout, gpm, gpf, wup, wconv, wdown, gpo]
    if per_tile_seqs:
        s1 = jnp.pad(conv_prev[:, 1:2], ((0, 0), (0, seq - 1), (0, 0)))
        s2 = jnp.pad(conv_prev, ((0, 0), (0, seq - 2), (0, 0)))
        in_specs += [row(w2), row(w2)]
        args += [s1.reshape(m, w2), s2.reshape(m, w2)]
        second = pl.BlockSpec((tm, w2), lambda i: (i, 0))
        second_shape = jax.ShapeDtypeStruct((m, w2), F32)
    else:
        in_specs += [pl.BlockSpec((1, CONV_W - 1, w2), lambda i: (i // tiles_per_seq, 0, 0))]
        args += [conv_prev]
        second = pl.BlockSpec((1, CONV_W - 1, w2), lambda i: (i // tiles_per_seq, 0, 0))
        second_shape = jax.ShapeDtypeStruct((n_seq, CONV_W - 1, w2), F32)
    est = ((D_MODEL * D_MODEL + D_MODEL * w2 + D_FF * D_MODEL) * 2
           + 6 * tm * w2 * 4 + 8 * tm * D_MODEL * 4 + (6 << 20))
    return pl.pallas_call(
        functools.partial(_post_kernel, tm=tm, seq_rows=seq, tiles_per_seq=tiles_per_seq),
        grid=(m // tm,),
        in_specs=in_specs,
        out_specs=[row(D_MODEL), second],
        out_shape=[jax.ShapeDtypeStruct((m, D_MODEL), F32), second_shape],
        scratch_shapes=[pltpu.VMEM((FFN_SLABS, tm + SUBLANES, LANES), F32),
                        pltpu.VMEM((SUBLANES, w2), F32)],
        compiler_params=pltpu.CompilerParams(
            dimension_semantics=("arbitrary",), vmem_limit_bytes=_vmem_limit(est)),
        name="post_mix_ffn",
    )(*args)


def _permute_w_in(w):
    r0 = IN_COLS - LIN_VW - GLA_RANK
    pad = jnp.zeros((w.shape[0], IN_COLS_PAD - IN_COLS), w.dtype)
    return jnp.concatenate([w[:, :r0], w[:, r0 + GLA_RANK:], w[:, r0:r0 + GLA_RANK], pad],
                           axis=1).astype(BF16)


def kernel(x_prompt, x_sample, cache_win_k, cache_win_v, state_hgrn, state_gla, state_conv,
           w_in, w_gate_up, b_gate_up, w_out, g_hgrn, g_gla, lb_logits,
           w_up, w_conv, w_down, g_pre_mix, g_post_mix, g_pre_ffn, g_post_ffn):
    nb, seq, d = x_prompt.shape
    db, t_new, _ = x_sample.shape
    win = cache_win_k.shape[2]
    tm = 256
    slopes = jnp.asarray(2.0 ** (-8.0 * np.arange(1, N_HEADS_ATT + 1) / N_HEADS_ATT), F32)
    ck = cache_win_k.reshape(N_LAYERS, db, win, ATT_W)
    cv = cache_win_v.reshape(N_LAYERS, db, win, ATT_W)

    xp = x_prompt.reshape(nb * seq, d)
    xs = x_sample.reshape(db * t_new, d)
    zero_hgrn = jnp.zeros((nb, N_HEADS_LIN, HGRN_K, LIN_V), F32)
    zero_gla = jnp.zeros((nb, N_HEADS_LIN, GLA_K, LIN_V), F32)
    zero_conv = jnp.zeros((nb, CONV_W - 1, 2 * D_FF), F32)
    keep = min(WIN_MAX, seq)

    wk_buf = wv_buf = None
    wkp, wvp, hg_p, hg_s, gl_p, gl_s, cv_p, cv_s = [], [], [], [], [], [], [], []
    for l in range(N_LAYERS):
        w_in_l = _permute_w_in(w_in[l])
        wg = jnp.zeros((LANES, GLA_KW), F32).at[:GLA_RANK].set(w_gate_up[l]).astype(BF16)
        bg = b_gate_up[l].reshape(1, GLA_KW)
        gpre = g_pre_mix[l].reshape(1, d)
        gh = g_hgrn[l].reshape(1, LIN_VW)
        gg = g_gla[l].reshape(1, LIN_VW)
        post_w = (w_out[l].astype(BF16), g_post_mix[l].reshape(1, d), g_pre_ffn[l].reshape(1, d),
                  w_up[l].astype(BF16), w_conv[l], w_down[l].astype(BF16), g_post_ffn[l].reshape(1, d))

        hp = _norm_matmul(xp, gpre, w_in_l, tm)
        oa = _attn_prompt(hp, slopes, nb, seq)
        ob, sh = _hgrn_scan(hp, lb_logits, gh, zero_hgrn, l, nb, seq)
        oc, sg = _gla_scan(hp, wg, bg, gg, zero_gla, nb, seq)
        xp, conv_p = _post_mix_ffn(oa, ob, oc, xp, *post_w, zero_conv, nb, seq, 2 * tm)
        h3 = hp.reshape(nb, seq, IN_COLS_PAD)
        wkp.append(h3[:, seq - keep:, ATT_W:2 * ATT_W])
        wvp.append(h3[:, seq - keep:, 2 * ATT_W:3 * ATT_W])
        hg_p.append(sh)
        gl_p.append(sg)
        cv_p.append(conv_p)

        hs = _norm_matmul(xs, gpre, w_in_l, db * t_new)
        oa, wk_buf, wv_buf = _attn_decode(hs, ck, cv, wk_buf, wv_buf, l, db, t_new, win)
        ob, sh = _hgrn_scan(hs, lb_logits, gh, state_hgrn[l], l, db, t_new)
        oc, sg = _gla_scan(hs, wg, bg, gg, state_gla[l], db, t_new)
        xs, u_s = _post_mix_ffn(oa, ob, oc, xs, *post_w, state_conv[l], db, t_new, db * t_new)
        hg_s.append(sh)
        gl_s.append(sg)
        cv_s.append(u_s.reshape(db, t_new, 2 * D_FF)[:, t_new - (CONV_W - 1):])

    hshape = (N_LAYERS, nb, keep, N_HEADS_ATT, HEAD_DIM)
    sshape = (N_LAYERS, db, win, N_HEADS_ATT, HEAD_DIM)
    return (xp.reshape(nb, seq, d), xs.reshape(db, t_new, d),
            jnp.stack(wkp).reshape(hshape), jnp.stack(wvp).reshape(hshape),
            wk_buf.reshape(sshape), wv_buf.reshape(sshape),
            jnp.stack(hg_p), jnp.stack(hg_s), jnp.stack(gl_p), jnp.stack(gl_s),
            jnp.stack(cv_p), jnp.stack(cv_s))
```

```python
import functools
import math

import numpy as np
import jax
import jax.numpy as jnp
from jax import lax
from jax.experimental import pallas as pl
from jax.experimental.pallas import tpu as pltpu

F32 = jnp.float32
BF16 = jnp.bfloat16

D_MODEL = 1024
N_LAYERS = 4
HEAD_DIM = 64
N_HEADS_ATT = 8
ATT_W = N_HEADS_ATT * HEAD_DIM
ATT_J = 128
ATT_DILATIONS = (1, 4, 16)
WIN_MAX = 2048
ATT_UNROLL = 4
N_HEADS_LIN = 4
HGRN_K = 128
GLA_K = 32
LIN_V = 64
LIN_VW = N_HEADS_LIN * LIN_V
HGRN_KW = N_HEADS_LIN * HGRN_K
GLA_KW = N_HEADS_LIN * GLA_K
GLA_RANK = 16
GLA_GATE_NORM = 16.0
D_FF = 2816
CONV_W = 3
RMS_EPS = 1e-6
IN_COLS = 3856
IN_COLS_PAD = 3968

LANES = 128
SUBLANES = 8
VMEM_LIMIT_CAP = 56 * 1024 * 1024

SCAN_C = 128
NEG_BIG = 1e30
LOG2E = 1.4426950408889634
SCAN_ROWS_PER_STEP = 512


def _vmem_limit(nbytes):
    return int(min(VMEM_LIMIT_CAP, max(32 * 1024 * 1024, nbytes)))


def _rms(x, g):
    ms = jnp.mean(x * x, axis=-1, keepdims=True)
    return x * lax.rsqrt(ms + RMS_EPS) * g


def _dot(a, b):
    return jnp.dot(a, b, preferred_element_type=F32)


def _dot_nt(a, b):
    return lax.dot_general(a, b, (((1,), (1,)), ((), ())), preferred_element_type=F32)


def _norm_matmul_kernel(x_ref, g_ref, w_ref, o_ref):
    xn = _rms(x_ref[...], g_ref[...]).astype(BF16)
    o_ref[...] = _dot(xn, w_ref[...])


def _norm_matmul(x, g, w, tm):
    m, d = x.shape
    n = w.shape[1]
    est = 2 * tm * d * 4 + 2 * d * n * 2 + 2 * tm * n * 4 + (4 << 20)
    return pl.pallas_call(
        _norm_matmul_kernel,
        grid=(m // tm,),
        in_specs=[
            pl.BlockSpec((tm, d), lambda i: (i, 0)),
            pl.BlockSpec((1, d), lambda i: (0, 0)),
            pl.BlockSpec((d, n), lambda i: (0, 0)),
        ],
        out_specs=pl.BlockSpec((tm, n), lambda i: (i, 0)),
        out_shape=jax.ShapeDtypeStruct((m, n), F32),
        compiler_params=pltpu.CompilerParams(
            dimension_semantics=("arbitrary",), vmem_limit_bytes=_vmem_limit(est)),
        name="norm_matmul",
    )(x, g, w)


def _attn_prompt_kernel(slopes_ref, q_ref, k_ref, v_ref, o_ref, m_s, l_s, acc_s, bias_s, *, seq):
    hp = pl.program_id(1)
    j = ATT_J
    qi = lax.broadcasted_iota(jnp.int32, (j, 2 * j), 0)
    ki = lax.broadcasted_iota(jnp.int32, (j, 2 * j), 1)
    d1 = j + qi - ki
    dt1 = jnp.where((d1 >= 0) & (d1 <= j), d1.astype(F32), NEG_BIG)
    d0 = qi - ki
    dt0 = jnp.where((ki < j) & (d0 >= 0), d0.astype(F32), NEG_BIG)
    for g, dil in enumerate(ATT_DILATIONS):
        for half in (0, 1):
            sc = slopes_ref[2 * hp + half] * float(dil)
            bias_s[g, half, 0] = -sc * dt0
            bias_s[g, half, 1] = -sc * dt1
    lane = lax.broadcasted_iota(jnp.int32, (1, LANES), 1)
    half1 = lane >= HEAD_DIM
    ones = jnp.ones((2 * j, LANES), BF16)

    def branch(g, dil, init):
        nb = seq // (j * dil)
        lognb = int(math.log2(nb))
        unroll = ATT_UNROLL
        assert unroll % nb == 0 or nb % unroll == 0

        def rows(start, size):
            return pl.ds(start, size) if dil == 1 else pl.ds(start, size, stride=dil)

        def load(idx, u):
            b = idx & (nb - 1)
            r = idx >> lognb
            start = b * (j * dil) + r
            if unroll % nb == 0:
                first = u % nb == 0
                kstart = r if first else start - j * dil
                sel = 0 if first else 1
            elif u == 0:
                first = b == 0
                kstart = jnp.where(first, r, start - j * dil)
                sel = jnp.where(first, 0, 1)
            else:
                kstart = start - j * dil
                sel = 1
            if dil == 1:
                start = pl.multiple_of(start, j)
                kstart = pl.multiple_of(kstart, j)
            q = q_ref[rows(start, j), :] * (HEAD_DIM ** -0.5)
            kk = k_ref[rows(kstart, 2 * j), :].astype(BF16)
            vv = v_ref[rows(kstart, 2 * j), :].astype(BF16)
            old = None
            if not init:
                old = (m_s[0, rows(start, j), :], m_s[1, rows(start, j), :],
                       l_s[0, rows(start, j), :], l_s[1, rows(start, j), :],
                       acc_s[rows(start, j), :])
            return start, sel, q, kk, vv, old

        def compute(sel, q, kk, vv, old):
            vx = jnp.concatenate([vv, ones], axis=1)
            m_out, l_out, acc_new = [], [], None
            for half in (0, 1):
                hmask = half1 if half else jnp.logical_not(half1)
                qh = jnp.where(hmask, q, 0.0).astype(BF16)
                s = _dot_nt(qh, kk) + bias_s[g, half, sel]
                mx = jnp.max(s, axis=-1, keepdims=True)
                if init:
                    m_new = jnp.broadcast_to(mx, (j, LANES))
                else:
                    m_new = jnp.maximum(old[half], mx)
                p = jnp.exp(s - jnp.concatenate([m_new, m_new], axis=1)).astype(BF16)
                pvx = _dot(p, vx)
                pv, rs = pvx[:, :LANES], pvx[:, LANES:]
                if init:
                    l_new, contrib = rs, pv
                else:
                    alpha = jnp.exp(old[half] - m_new)
                    l_new = alpha * old[2 + half] + rs
                    contrib = alpha * old[4] + pv
                m_out.append(m_new)
                l_out.append(l_new)
                acc_new = contrib if half == 0 else jnp.where(half1, contrib, acc_new)
            return m_out, l_out, acc_new

        def body(it, carry):
            loaded = [load(it * unroll + u, u) for u in range(unroll)]
            done = [(ld[0],) + compute(*ld[1:]) for ld in loaded]
            for start, m_out, l_out, acc_new in done:
                for half in (0, 1):
                    m_s[half, rows(start, j), :] = m_out[half]
                    l_s[half, rows(start, j), :] = l_out[half]
                acc_s[rows(start, j), :] = acc_new
            return carry

        lax.fori_loop(0, nb * dil // unroll, body, 0)

    order = sorted(range(len(ATT_DILATIONS)), key=lambda g: -ATT_DILATIONS[g])
    for n, g in enumerate(order):
        branch(g, ATT_DILATIONS[g], n == 0)

    def finish(c, carry):
        rows = pl.ds(pl.multiple_of(c * j, j), j)
        l = jnp.where(half1, l_s[1, rows, :], l_s[0, rows, :])
        o_ref[rows, :] = acc_s[rows, :] / l
        return carry

    lax.fori_loop(0, seq // j, finish, 0)


def _attn_prompt(h, slopes, n_seq, seq):
    npairs = ATT_W // LANES
    blk = (seq, LANES)
    est = 5 * 2 * seq * LANES * 4 + 5 * seq * LANES * 4 + (8 << 20)
    return pl.pallas_call(
        functools.partial(_attn_prompt_kernel, seq=seq),
        grid=(n_seq, npairs),
        in_specs=[
            pl.BlockSpec(memory_space=pltpu.SMEM),
            pl.BlockSpec(blk, lambda n, p: (n, p)),
            pl.BlockSpec(blk, lambda n, p: (n, npairs + p)),
            pl.BlockSpec(blk, lambda n, p: (n, 2 * npairs + p)),
        ],
        out_specs=pl.BlockSpec(blk, lambda n, p: (n, p)),
        out_shape=jax.ShapeDtypeStruct((n_seq * seq, ATT_W), F32),
        scratch_shapes=[
            pltpu.VMEM((2, seq, LANES), F32),
            pltpu.VMEM((2, seq, LANES), F32),
            pltpu.VMEM((seq, LANES), F32),
            pltpu.VMEM((len(ATT_DILATIONS), 2, 2, ATT_J, 2 * ATT_J), F32),
        ],
        compiler_params=pltpu.CompilerParams(
            dimension_semantics=("arbitrary", "arbitrary"), vmem_limit_bytes=_vmem_limit(est)),
        name="attn_prompt",
    )(slopes, h, h, h)


def _attn_decode_tables(t_new, win):
    cols = win + LANES
    c = np.arange(cols)[None, :]
    i = np.arange(t_new)[:, None]
    delta = win + i - c
    ok = (delta >= 0) & (c < win + t_new)
    mult = np.zeros((t_new, cols), np.float32)
    for dil in ATT_DILATIONS:
        mult += (ok & (delta % dil == 0) & (delta <= ATT_J * dil)).astype(np.float32)
    slopes = 2.0 ** (-8.0 * np.arange(1, N_HEADS_ATT + 1) / N_HEADS_ATT)
    bias = -slopes[:, None, None] * delta[None].astype(np.float64)
    bias = np.where(mult[None] > 0, bias, -NEG_BIG).astype(np.float32)
    bias = bias.reshape(N_HEADS_ATT * t_new, cols)
    mult = np.tile(mult, (N_HEADS_ATT, 1))
    return bias, mult


def _attn_decode_kernel(bias_ref, mult_ref, q_ref, k_ref, v_ref, ck_ref, cv_ref,
                        o_ref, wk_ref, wv_ref, *, t_new, win):
    lane = lax.broadcasted_iota(jnp.int32, (1, LANES), 1)
    half1 = lane >= HEAD_DIM
    q = q_ref[...] * (HEAD_DIM ** -0.5)
    qs = jnp.concatenate([jnp.where(half1, 0.0, q), jnp.where(half1, q, 0.0)], axis=0).astype(BF16)
    pad = jnp.zeros((LANES - t_new, LANES), F32)
    kt_new = jnp.concatenate([k_ref[...], pad], axis=0).T
    vt_new = jnp.concatenate([v_ref[...], pad], axis=0).T
    ck = ck_ref[...]
    cv = cv_ref[...]
    s = jnp.concatenate([_dot(qs, ck.astype(BF16)), _dot(qs, kt_new.astype(BF16))], axis=1)
    s = s + bias_ref[...]
    mx = jnp.max(s, axis=-1, keepdims=True)
    p = mult_ref[...] * jnp.exp(s - mx)
    l = jnp.sum(p, axis=-1, keepdims=True)
    pb = p.astype(BF16)
    o_all = (_dot_nt(pb[:, :win], cv.astype(BF16)) + _dot_nt(pb[:, win:], vt_new.astype(BF16))) / l
    o_ref[...] = jnp.where(half1, o_all[t_new:], o_all[:t_new])
    for c_ref, new, w_ref in ((ck_ref, kt_new, wk_ref), (cv_ref, vt_new, wv_ref)):
        shifted = pltpu.roll(c_ref[...], win - t_new, axis=1)
        tail = jnp.where(lane >= LANES - t_new, pltpu.roll(new, LANES - t_new, axis=1),
                         shifted[:, win - LANES:])
        w_ref[:, 0:win - LANES] = shifted[:, 0:win - LANES]
        w_ref[:, win - LANES:] = tail


def _attn_decode(h, cache_k, cache_v, wk_buf, wv_buf, layer, n_seq, t_new, win):
    bias, mult = _attn_decode_tables(t_new, win)
    npairs = ATT_W // LANES
    rows = 2 * t_new
    cols = win + LANES
    tab = pl.BlockSpec((rows, cols), lambda n, p: (p, 0))
    tok = lambda c: pl.BlockSpec((t_new, LANES), lambda n, p, c=c: (n, c * npairs + p))
    cache = pl.BlockSpec((None, None, LANES, win), lambda n, p: (layer, n, p, 0))
    in_specs = [tab, tab, tok(0), tok(1), tok(2), cache, cache]
    args = [jnp.asarray(bias), jnp.asarray(mult), h, h, h, cache_k, cache_v]
    aliases = {}
    if wk_buf is not None:
        in_specs += [pl.BlockSpec(memory_space=pl.ANY), pl.BlockSpec(memory_space=pl.ANY)]
        args += [wk_buf, wv_buf]
        aliases = {7: 1, 8: 2}
    kernel = functools.partial(_attn_decode_kernel, t_new=t_new, win=win)
    if wk_buf is not None:
        kernel = lambda *refs, _k=kernel: _k(*refs[:7], *refs[9:])
    win_shape = jax.ShapeDtypeStruct((N_LAYERS, n_seq, ATT_W, win), F32)
    est = 16 * LANES * win * 4 + (8 << 20)
    return pl.pallas_call(
        kernel,
        grid=(n_seq, npairs),
        in_specs=in_specs,
        out_specs=[pl.BlockSpec((t_new, LANES), lambda n, p: (n, p)), cache, cache],
        out_shape=[jax.ShapeDtypeStruct((n_seq * t_new, ATT_W), F32), win_shape, win_shape],
        input_output_aliases=aliases,
        compiler_params=pltpu.CompilerParams(
            dimension_semantics=("arbitrary", "arbitrary"), vmem_limit_bytes=_vmem_limit(est)),
        name="attn_decode",
    )(*args)


SCAN_MXU_LEVELS = 3


def _scan_tables(c):
    levels = int(math.log2(c))
    t = np.arange(c)[:, None]
    u = np.arange(c)[None, :]
    mats, masks = [], [np.eye(c, dtype=np.float32)]
    for li in range(levels):
        m = 1 << li
        bnd = (t // (2 * m)) * 2 * m + m - 1
        right = (t % (2 * m)) >= m
        if li < SCAN_MXU_LEVELS:
            mats.append(np.where(right, (u > bnd) & (u <= t), (u > t) & (u <= bnd)))
        masks.append(((t // (2 * m)) == (u // (2 * m))) & right & ((u % (2 * m)) < m))
    small = np.concatenate(mats, axis=0).astype(np.float32)
    tri = (u <= t).astype(np.float32)
    return (jnp.asarray(np.concatenate([small, small], axis=1), BF16),
            jnp.asarray(np.concatenate([tri, tri, tri], axis=1), BF16),
            jnp.asarray(np.stack(masks).astype(np.float32)))


def _scan_exponents(g, small_ref, tri_ref, c):
    levels = int(math.log2(c))
    g1 = g.astype(BF16)
    r1 = g - g1.astype(F32)
    g2 = r1.astype(BF16)
    g3 = (r1 - g2.astype(F32)).astype(BF16)
    small = jnp.exp2(_dot(small_ref[...], jnp.concatenate([g1, g2], axis=0)))
    e = [small[li * c:(li + 1) * c] for li in range(SCAN_MXU_LEVELS)]
    b = _dot(tri_ref[...], jnp.concatenate([g1, g2, g3], axis=0))
    row = lax.broadcasted_iota(jnp.int32, (c, 1), 0)
    for li in range(SCAN_MXU_LEVELS, levels):
        m = 1 << li
        bnd = jnp.concatenate(
            [jnp.broadcast_to(b[blk * 2 * m + m - 1:blk * 2 * m + m], (2 * m, LANES))
             for blk in range(c // (2 * m))], axis=0)
        sign = jnp.where((row & (2 * m - 1)) >= m, 1.0, -1.0)
        e.append(jnp.exp2((b - bnd) * sign))
    return e, jnp.exp2(b), jnp.exp2(b[c - 1:c] - b)


def _scan_scores(qs, kk, e, masks_ref, c, qmask=None):
    levels = int(math.log2(c))
    row = lax.broadcasted_iota(jnp.int32, (c, 1), 0)

    def sel(x):
        return x if qmask is None else jnp.where(qmask, x, 0.0)

    a = masks_ref[0] * _dot_nt(sel(qs).astype(BF16), kk.astype(BF16))
    for li in range(levels):
        m = 1 << li
        right = (row & (2 * m - 1)) >= m
        w = jnp.where(right, qs, kk) * e[li]
        a = a + masks_ref[li + 1] * _dot_nt(sel(w).astype(BF16), w.astype(BF16))
    return a


def _pad_rows(x, c):
    if x.shape[0] == c:
        return x
    return jnp.concatenate([x, jnp.zeros((c - x.shape[0], x.shape[1]), x.dtype)], axis=0)


def _gated_head_norm(o, gate, g):
    lane = lax.broadcasted_iota(jnp.int32, (1, LANES), 1)
    half1 = lane >= LIN_V
    o2 = o * o
    parts = []
    for p in range(LIN_VW // LANES):
        o2p = o2[:, p * LANES:(p + 1) * LANES]
        s_all = jnp.sum(o2p, axis=-1, keepdims=True)
        s_hi = jnp.sum(jnp.where(half1, o2p, 0.0), axis=-1, keepdims=True)
        parts.append(jnp.where(half1, s_hi, s_all - s_hi))
    ms = jnp.concatenate(parts, axis=1) * (1.0 / LIN_V)
    on = o * lax.rsqrt(ms + RMS_EPS)
    return on * g * (gate * (1.0 / (1.0 + jnp.exp(-gate))))


def _hgrn_kernel(q_ref, f_ref, i_ref, og_ref, lbl_ref, gh_ref, s0_ref, small_ref, tri_ref, masks_ref,
                 o_ref, sout_ref, st_s, *, layer, c, c_in, cps, n_steps):
    si = pl.program_id(1)

    @pl.when(si == 0)
    def _():
        st_s[...] = s0_ref[0]

    lbl = lbl_ref[...]
    ex = jnp.exp(lbl - jnp.max(lbl, axis=0, keepdims=True))
    prob = ex / jnp.sum(ex, axis=0, keepdims=True)
    lb = jnp.zeros((1, HGRN_KW), F32)
    for r in range(1, layer + 1):
        lb = lb + prob[r:r + 1]
    lane = lax.broadcasted_iota(jnp.int32, (1, LANES), 1)
    half1 = lane >= LIN_V

    def chunk(rows):
        q = _pad_rows(q_ref[rows, :], c)
        f = _pad_rows(f_ref[rows, :], c)
        vi = _pad_rows(i_ref[rows, :], c)
        forget = lb + (1.0 - lb) * (1.0 / (1.0 + jnp.exp(-f)))
        g = jnp.log2(forget)
        k = 1.0 - forget
        if c_in < c:
            valid = lax.broadcasted_iota(jnp.int32, (c, 1), 0) < c_in
            g = jnp.where(valid, g, 0.0)
            k = jnp.where(valid, k, 0.0)
        pairs = []
        for p in range(N_HEADS_LIN // 2):
            v = vi[:, p * LANES:(p + 1) * LANES]
            vb = v.astype(BF16)
            vt = v.T.astype(BF16)
            o_pair = None
            for half in range(2):
                hd = 2 * p + half
                cols = slice(hd * HGRN_K, (hd + 1) * HGRN_K)
                qs = q[:, cols] * (HGRN_K ** -0.5)
                kk = k[:, cols]
                e, eb, eu = _scan_exponents(g[:, cols], small_ref, tri_ref, c)
                a = _scan_scores(qs, kk, e, masks_ref, c)
                st = st_s[hd]
                o_h = _dot(a.astype(BF16), vb) + _dot_nt((qs * eb).astype(BF16), st.astype(BF16))
                st_s[hd] = st * eb[c - 1:c, :] + _dot(vt, (kk * eu).astype(BF16))
                o_pair = o_h if half == 0 else jnp.where(half1, o_h, o_pair)
            pairs.append(o_pair)
        o = jnp.concatenate(pairs, axis=1)
        out = _gated_head_norm(o, _pad_rows(og_ref[rows, :], c), gh_ref[...])
        o_ref[rows, :] = out[:c_in]

    _for_chunks(chunk, cps, c_in)

    @pl.when(si == n_steps - 1)
    def _():
        sout_ref[0] = st_s[...]


def _for_chunks(chunk, cps, c_in):
    if cps == 1:
        chunk(slice(None))
    else:
        def body(ci, carry):
            chunk(pl.ds(pl.multiple_of(ci * c_in, c_in), c_in))
            return carry
        lax.fori_loop(0, cps, body, 0)


def _gla_kernel(q_ref, k_ref, v_ref, r_ref, og_ref, wg_ref, bg_ref, gg_ref, s0_ref, bd_ref,
                small_ref, tri_ref, masks_ref, o_ref, sout_ref, st_s, *, c, c_in, cps, n_steps):
    si = pl.program_id(1)

    @pl.when(si == 0)
    def _():
        st_s[...] = s0_ref[0]

    klane = lax.broadcasted_iota(jnp.int32, (1, GLA_KW), 1) // GLA_K
    vlane = lax.broadcasted_iota(jnp.int32, (1, LIN_VW), 1) // LIN_V

    def chunk(rows):
        qs = _pad_rows(q_ref[rows, :], c) * (GLA_K ** -0.5)
        kk = _pad_rows(k_ref[rows, :], c)
        v = _pad_rows(v_ref[rows, :], c)
        x = _dot(_pad_rows(r_ref[rows, :], c).astype(BF16), wg_ref[...]) + bg_ref[...]
        g = (jnp.minimum(x, 0.0) - jnp.log(1.0 + jnp.exp(-jnp.abs(x)))) * (LOG2E / GLA_GATE_NORM)
        if c_in < c:
            valid = lax.broadcasted_iota(jnp.int32, (c, 1), 0) < c_in
            g = jnp.where(valid, g, 0.0)
            kk = jnp.where(valid, kk, 0.0)
        e, eb, eu = _scan_exponents(g, small_ref, tri_ref, c)
        st = st_s[...]
        vb = v.astype(BF16)
        o = _dot_nt((qs * eb).astype(BF16), st.astype(BF16))
        for hd in range(N_HEADS_LIN):
            a = _scan_scores(qs, kk, e, masks_ref, c, qmask=klane == hd)
            o = o + jnp.where(vlane == hd, _dot(a.astype(BF16), vb), 0.0)
        upd = _dot(v.T.astype(BF16), (kk * eu).astype(BF16))
        st_s[...] = (st * eb[c - 1:c, :] + upd) * bd_ref[...]
        out = _gated_head_norm(o, _pad_rows(og_ref[rows, :], c), gg_ref[...])
        o_ref[rows, :] = out[:c_in]

    _for_chunks(chunk, cps, c_in)

    @pl.when(si == n_steps - 1)
    def _():
        sout_ref[0] = st_s[...]


def _scan_call(kernel_fn, name, h, tok_cols, consts_before, s0k, consts_after, n_seq, seq, **static):
    c = SCAN_C
    c_in = min(seq, c)
    cps = max(min(seq, SCAN_ROWS_PER_STEP) // c, 1)
    n_steps = seq // (c_in * cps)
    rows = c_in * cps
    tables = _scan_tables(c)
    tok = [pl.BlockSpec((rows, w), lambda n, si, col=col: (n * n_steps + si, col))
           for (w, col) in tok_cols]
    cst = lambda x: pl.BlockSpec(x.shape, lambda n, si, nd=x.ndim: (0,) * nd)
    state_shape = s0k.shape[1:]
    state = pl.BlockSpec((1,) + state_shape, lambda n, si: (n,) + (0,) * len(state_shape))
    consts_after = list(consts_after) + list(tables)
    return pl.pallas_call(
        functools.partial(kernel_fn, c=c, c_in=c_in, cps=cps, n_steps=n_steps, **static),
        grid=(n_seq, n_steps),
        in_specs=tok + [cst(x) for x in consts_before] + [state] + [cst(x) for x in consts_after],
        out_specs=[pl.BlockSpec((rows, LIN_VW), lambda n, si: (n * n_steps + si, 0)), state],
        out_shape=[jax.ShapeDtypeStruct((n_seq * seq, LIN_VW), F32),
                   jax.ShapeDtypeStruct((n_seq,) + state_shape, F32)],
        scratch_shapes=[pltpu.VMEM(state_shape, F32)],
        compiler_params=pltpu.CompilerParams(
            dimension_semantics=("arbitrary", "arbitrary"),
            vmem_limit_bytes=_vmem_limit(48 << 20)),
        name=name,
    )(*([h] * len(tok_cols)), *consts_before, s0k, *consts_after)


def _hgrn_scan(h, lb_logits, g_hgrn, s0, layer, n_seq, seq):
    s0t = jnp.swapaxes(s0, 2, 3)
    zero = jnp.zeros_like(s0t)
    even = jnp.concatenate([s0t, zero], axis=2)
    odd = jnp.concatenate([zero, s0t], axis=2)
    is_odd = (jnp.arange(N_HEADS_LIN) % 2 == 1)[None, :, None, None]
    s0k = jnp.where(is_odd, odd, even)
    tok_cols = [(HGRN_KW, 3), (HGRN_KW, 4), (LIN_VW, 10), (LIN_VW, 11)]
    o, sk = _scan_call(_hgrn_kernel, "hgrn_scan", h, tok_cols, [lb_logits, g_hgrn], s0k, [],
                       n_seq, seq, layer=layer)
    sk = sk.reshape(n_seq, N_HEADS_LIN, 2, LIN_V, HGRN_K)
    s_even, s_odd = sk[:, :, 0], sk[:, :, 1]
    st = jnp.where(is_odd, s_odd, s_even)
    return o, jnp.swapaxes(st, 2, 3)


def _gla_scan(h, wg, bg, g_gla, s0, n_seq, seq):
    bd = (np.arange(LIN_VW)[:, None] // LIN_V == np.arange(GLA_KW)[None, :] // GLA_K)
    bd = jnp.asarray(bd.astype(np.float32))
    s0t = jnp.swapaxes(s0, 2, 3)
    eye = jnp.eye(N_HEADS_LIN, dtype=F32)[None, :, None, :, None]
    s0k = (s0t[:, :, :, None, :] * eye).reshape(n_seq, LIN_VW, GLA_KW)
    tok_cols = [(GLA_KW, 24), (GLA_KW, 25), (LIN_VW, 13), (LANES, 30), (LIN_VW, 14)]
    o, sk = _scan_call(_gla_kernel, "gla_scan", h, tok_cols, [wg, bg, g_gla], s0k, [bd], n_seq, seq)
    st = jnp.stack([sk[:, hd * LIN_V:(hd + 1) * LIN_V, hd * GLA_K:(hd + 1) * GLA_K]
                    for hd in range(N_HEADS_LIN)], axis=1)
    return o, jnp.swapaxes(st, 2, 3)


def _gelu_tanh(x):
    return x * (0.5 * (1.0 + jnp.tanh(math.sqrt(2.0 / math.pi) * (x + 0.044715 * (x * x * x)))))


FFN_CHUNK = 256
FFN_SLABS = 8


def _post_kernel(*refs, tm, seq_rows, tiles_per_seq):
    per_tile_seqs = seq_rows < tm
    if per_tile_seqs:
        (oa_ref, ob_ref, oc_ref, x_ref, wout_ref, gpm_ref, gpf_ref, wup_ref, wconv_ref,
         wdown_ref, gpo_ref, s1_ref, s2_ref, xo_ref, u_ref, slab_s, halo_s) = refs
    else:
        (oa_ref, ob_ref, oc_ref, x_ref, wout_ref, gpm_ref, gpf_ref, wup_ref, wconv_ref,
         wdown_ref, gpo_ref, prev_ref, xo_ref, conv_ref, slab_s, halo_s) = refs
    i = pl.program_id(0)
    w2 = 2 * D_FF
    h0 = SUBLANES
    mix = jnp.concatenate([oa_ref[...], ob_ref[...], oc_ref[...]], axis=1).astype(BF16)
    x1 = x_ref[...] + _rms(_dot(mix, wout_ref[...]), gpm_ref[...])
    xn = _rms(x1, gpf_ref[...]).astype(BF16)
    if per_tile_seqs:
        halo_s[...] = jnp.zeros((h0, w2), F32)
        t = lax.broadcasted_iota(jnp.int32, (tm, 1), 0) % seq_rows
    else:
        @pl.when(i % tiles_per_seq == 0)
        def _():
            halo_s[0:h0 - 2, :] = jnp.zeros((h0 - 2, w2), F32)
            halo_s[h0 - 2:h0, :] = prev_ref[0]

    def conv_tile(u_tile, col, ti):
        cols = slice(col, col + LANES)
        slab_s[ti, 0:h0, :] = halo_s[:, cols]
        slab_s[ti, h0:h0 + tm, :] = u_tile
        sh1 = slab_s[ti, h0 - 1:h0 - 1 + tm, :]
        sh2 = slab_s[ti, h0 - 2:h0 - 2 + tm, :]
        if per_tile_seqs:
            sh1 = jnp.where(t >= 1, sh1, s1_ref[:, cols])
            sh2 = jnp.where(t >= 2, sh2, s2_ref[:, cols])
            u_ref[:, cols] = u_tile
        else:
            conv_ref[0, :, cols] = u_tile[tm - 2:tm]
            halo_s[:, cols] = u_tile[tm - h0:tm]
        return sh2 * wconv_ref[0:1, cols] + sh1 * wconv_ref[1:2, cols] + u_tile * wconv_ref[2:3, cols]

    ffn = jnp.zeros((tm, D_MODEL), F32)
    tiles = FFN_CHUNK // LANES
    for ci, c0 in enumerate(range(0, D_FF, FFN_CHUNK)):
        parts = []
        for pi, base in enumerate((c0, D_FF + c0)):
            u = _dot(xn, wup_ref[:, base:base + FFN_CHUNK])
            slab0 = ((ci % 2) * 2 + pi) * tiles
            parts.append(jnp.concatenate(
                [conv_tile(u[:, k * LANES:(k + 1) * LANES], base + k * LANES, slab0 + k)
                 for k in range(tiles)], axis=1))
        act = (_gelu_tanh(parts[0]) * parts[1]).astype(BF16)
        ffn = ffn + _dot(act, wdown_ref[c0:c0 + FFN_CHUNK, :])
    xo_ref[...] = x1 + _rms(ffn, gpo_ref[...])


def _post_mix_ffn(oa, ob, oc, x, wout, gpm, gpf, wup, wconv, wdown, gpo, conv_prev, n_seq, seq, tm):
    m = x.shape[0]
    w2 = 2 * D_FF
    per_tile_seqs = seq < tm
    tiles_per_seq = max(seq // tm, 1)
    row = lambda w: pl.BlockSpec((tm, w), lambda i: (i, 0))
    cst = lambda shape: pl.BlockSpec(shape, lambda i, nd=len(shape): (0,) * nd,
                                     pipeline_mode=pl.Buffered(1))
    in_specs = [row(ATT_W), row(LIN_VW), row(LIN_VW), row(D_MODEL),
                cst((D_MODEL, D_MODEL)), cst((1, D_MODEL)), cst((1, D_MODEL)),
                cst((D_MODEL, w2)), cst((CONV_W, w2)), cst((D_FF, D_MODEL)), cst((1, D_MODEL))]
    args = [oa, ob, oc, x, wout, gpm, gpf, wup, wconv, wdown, gpo]
    if per_tile_seqs:
        s1 = jnp.pad(conv_prev[:, 1:2], ((0, 0), (0, seq - 1), (0, 0)))
        s2 = jnp.pad(conv_prev, ((0, 0), (0, seq - 2), (0, 0)))
        in_specs += [row(w2), row(w2)]
        args += [s1.reshape(m, w2), s2.reshape(m, w2)]
        second = pl.BlockSpec((tm, w2), lambda i: (i, 0))
        second_shape = jax.ShapeDtypeStruct((m, w2), F32)
    else:
        in_specs += [pl.BlockSpec((1, CONV_W - 1, w2), lambda i: (i // tiles_per_seq, 0, 0))]
        args += [conv_prev]
        second = pl.BlockSpec((1, CONV_W - 1, w2), lambda i: (i // tiles_per_seq, 0, 0))
        second_shape = jax.ShapeDtypeStruct((n_seq, CONV_W - 1, w2), F32)
    est = ((D_MODEL * D_MODEL + D_MODEL * w2 + D_FF * D_MODEL) * 2
           + 6 * tm * w2 * 4 + 8 * tm * D_MODEL * 4 + (6 << 20))
    return pl.pallas_call(
        functools.partial(_post_kernel, tm=tm, seq_rows=seq, tiles_per_seq=tiles_per_seq),
        grid=(m // tm,),
        in_specs=in_specs,
        out_specs=[row(D_MODEL), second],
        out_shape=[jax.ShapeDtypeStruct((m, D_MODEL), F32), second_shape],
        scratch_shapes=[pltpu.VMEM((FFN_SLABS, tm + SUBLANES, LANES), F32),
                        pltpu.VMEM((SUBLANES, w2), F32)],
        compiler_params=pltpu.CompilerParams(
            dimension_semantics=("arbitrary",), vmem_limit_bytes=_vmem_limit(est)),
        name="post_mix_ffn",
    )(*args)


def _permute_w_in(w):
    r0 = IN_COLS - LIN_VW - GLA_RANK
    pad = jnp.zeros((w.shape[0], IN_COLS_PAD - IN_COLS), w.dtype)
    return jnp.concatenate([w[:, :r0], w[:, r0 + GLA_RANK:], w[:, r0:r0 + GLA_RANK], pad],
                           axis=1).astype(BF16)


def kernel(x_prompt, x_sample, cache_win_k, cache_win_v, state_hgrn, state_gla, state_conv,
           w_in, w_gate_up, b_gate_up, w_out, g_hgrn, g_gla, lb_logits,
           w_up, w_conv, w_down, g_pre_mix, g_post_mix, g_pre_ffn, g_post_ffn):
    nb, seq, d = x_prompt.shape
    db, t_new, _ = x_sample.shape
    win = cache_win_k.shape[2]
    tm = 256
    slopes = jnp.asarray(2.0 ** (-8.0 * np.arange(1, N_HEADS_ATT + 1) / N_HEADS_ATT), F32)
    ck = jnp.transpose(cache_win_k, (0, 1, 3, 4, 2)).reshape(N_LAYERS, db, ATT_W, win)
    cv = jnp.transpose(cache_win_v, (0, 1, 3, 4, 2)).reshape(N_LAYERS, db, ATT_W, win)

    xp = x_prompt.reshape(nb * seq, d)
    xs = x_sample.reshape(db * t_new, d)
    zero_hgrn = jnp.zeros((nb, N_HEADS_LIN, HGRN_K, LIN_V), F32)
    zero_gla = jnp.zeros((nb, N_HEADS_LIN, GLA_K, LIN_V), F32)
    zero_conv = jnp.zeros((nb, CONV_W - 1, 2 * D_FF), F32)
    keep = min(WIN_MAX, seq)

    wk_buf = wv_buf = None
    wkp, wvp, hg_p, hg_s, gl_p, gl_s, cv_p, cv_s = [], [], [], [], [], [], [], []
    for l in range(N_LAYERS):
        w_in_l = _permute_w_in(w_in[l])
        wg = jnp.zeros((LANES, GLA_KW), F32).at[:GLA_RANK].set(w_gate_up[l]).astype(BF16)
        bg = b_gate_up[l].reshape(1, GLA_KW)
        gpre = g_pre_mix[l].reshape(1, d)
        gh = g_hgrn[l].reshape(1, LIN_VW)
        gg = g_gla[l].reshape(1, LIN_VW)
        post_w = (w_out[l].astype(BF16), g_post_mix[l].reshape(1, d), g_pre_ffn[l].reshape(1, d),
                  w_up[l].astype(BF16), w_conv[l], w_down[l].astype(BF16), g_post_ffn[l].reshape(1, d))

        hp = _norm_matmul(xp, gpre, w_in_l, tm)
        oa = _attn_prompt(hp, slopes, nb, seq)
        ob, sh = _hgrn_scan(hp, lb_logits, gh, zero_hgrn, l, nb, seq)
        oc, sg = _gla_scan(hp, wg, bg, gg, zero_gla, nb, seq)
        xp, conv_p = _post_mix_ffn(oa, ob, oc, xp, *post_w, zero_conv, nb, seq, 2 * tm)
        h3 = hp.reshape(nb, seq, IN_COLS_PAD)
        wkp.append(h3[:, seq - keep:, ATT_W:2 * ATT_W])
        wvp.append(h3[:, seq - keep:, 2 * ATT_W:3 * ATT_W])
        hg_p.append(sh)
        gl_p.append(sg)
        cv_p.append(conv_p)

        hs = _norm_matmul(xs, gpre, w_in_l, db * t_new)
        oa, wk_buf, wv_buf = _attn_decode(hs, ck, cv, wk_buf, wv_buf, l, db, t_new, win)
        ob, sh = _hgrn_scan(hs, lb_logits, gh, state_hgrn[l], l, db, t_new)
        oc, sg = _gla_scan(hs, wg, bg, gg, state_gla[l], db, t_new)
        xs, u_s = _post_mix_ffn(oa, ob, oc, xs, *post_w, state_conv[l], db, t_new, db * t_new)
        hg_s.append(sh)
        gl_s.append(sg)
        cv_s.append(u_s.reshape(db, t_new, 2 * D_FF)[:, t_new - (CONV_W - 1):])

    hshape = (N_LAYERS, nb, keep, N_HEADS_ATT, HEAD_DIM)

    def token_major(w):
        w = w.reshape(N_LAYERS, db, N_HEADS_ATT, HEAD_DIM, win)
        return jnp.transpose(w, (0, 1, 4, 2, 3))
    return (xp.reshape(nb, seq, d), xs.reshape(db, t_new, d),
            jnp.stack(wkp).reshape(hshape), jnp.stack(wvp).reshape(hshape),
            token_major(wk_buf), token_major(wv_buf),
            jnp.stack(hg_p), jnp.stack(hg_s), jnp.stack(gl_p), jnp.stack(gl_s),
            jnp.stack(cv_p), jnp.stack(cv_s))
```

```python
import functools
import math

import numpy as np
import jax
import jax.numpy as jnp
from jax import lax
from jax.experimental import pallas as pl
from jax.experimental.pallas import tpu as pltpu

F32 = jnp.float32
BF16 = jnp.bfloat16

D_MODEL = 1024
N_LAYERS = 4
HEAD_DIM = 64
N_HEADS_ATT = 8
ATT_W = N_HEADS_ATT * HEAD_DIM
ATT_J = 128
ATT_DILATIONS = (1, 4, 16)
WIN_MAX = 2048
ATT_UNROLL = 8
ATT_DECODE_PAIRS = 2
N_HEADS_LIN = 4
HGRN_K = 128
GLA_K = 32
LIN_V = 64
LIN_VW = N_HEADS_LIN * LIN_V
HGRN_KW = N_HEADS_LIN * HGRN_K
GLA_KW = N_HEADS_LIN * GLA_K
GLA_RANK = 16
GLA_GATE_NORM = 16.0
D_FF = 2816
CONV_W = 3
RMS_EPS = 1e-6
IN_COLS = 3856
IN_COLS_PAD = 3968

LANES = 128
SUBLANES = 8
VMEM_LIMIT_CAP = 56 * 1024 * 1024

SCAN_C = 128
NEG_BIG = 1e30
LOG2E = 1.4426950408889634
SCAN_ROWS_PER_STEP = 512


def _vmem_limit(nbytes):
    return int(min(VMEM_LIMIT_CAP, max(32 * 1024 * 1024, nbytes)))


def _rms(x, g):
    ms = jnp.mean(x * x, axis=-1, keepdims=True)
    return x * lax.rsqrt(ms + RMS_EPS) * g


def _dot(a, b):
    return jnp.dot(a, b, preferred_element_type=F32)


def _dot_nt(a, b):
    return lax.dot_general(a, b, (((1,), (1,)), ((), ())), preferred_element_type=F32)


def _norm_matmul_kernel(x_ref, g_ref, w_ref, o_ref):
    xn = _rms(x_ref[...], g_ref[...]).astype(BF16)
    o_ref[...] = _dot(xn, w_ref[...])


def _norm_matmul(x, g, w, tm):
    m, d = x.shape
    n = w.shape[1]
    est = 2 * tm * d * 4 + 2 * d * n * 2 + 2 * tm * n * 4 + (4 << 20)
    return pl.pallas_call(
        _norm_matmul_kernel,
        grid=(m // tm,),
        in_specs=[
            pl.BlockSpec((tm, d), lambda i: (i, 0)),
            pl.BlockSpec((1, d), lambda i: (0, 0)),
            pl.BlockSpec((d, n), lambda i: (0, 0)),
        ],
        out_specs=pl.BlockSpec((tm, n), lambda i: (i, 0)),
        out_shape=jax.ShapeDtypeStruct((m, n), F32),
        compiler_params=pltpu.CompilerParams(
            dimension_semantics=("arbitrary",), vmem_limit_bytes=_vmem_limit(est)),
        name="norm_matmul",
    )(x, g, w)


def _attn_prompt_kernel(slopes_ref, q_ref, k_ref, v_ref, o_ref, m_s, l_s, acc_s, bias_s, *, seq):
    hp = pl.program_id(1)
    j = ATT_J
    qi = lax.broadcasted_iota(jnp.int32, (j, 2 * j), 0)
    ki = lax.broadcasted_iota(jnp.int32, (j, 2 * j), 1)
    d1 = j + qi - ki
    dt1 = jnp.where((d1 >= 0) & (d1 <= j), d1.astype(F32), NEG_BIG)
    d0 = qi - ki
    dt0 = jnp.where((ki < j) & (d0 >= 0), d0.astype(F32), NEG_BIG)
    for g, dil in enumerate(ATT_DILATIONS):
        for half in (0, 1):
            sc = slopes_ref[2 * hp + half] * float(dil)
            bias_s[g, half, 0] = -sc * dt0
            bias_s[g, half, 1] = -sc * dt1
    lane = lax.broadcasted_iota(jnp.int32, (1, LANES), 1)
    half1 = lane >= HEAD_DIM
    ones = jnp.ones((2 * j, LANES), BF16)

    def branch(g, dil, init):
        nb = seq // (j * dil)
        lognb = int(math.log2(nb))
        unroll = ATT_UNROLL
        assert unroll % nb == 0 or nb % unroll == 0

        def rows(start, size):
            return pl.ds(start, size) if dil == 1 else pl.ds(start, size, stride=dil)

        def load(idx, u):
            b = idx & (nb - 1)
            r = idx >> lognb
            start = b * (j * dil) + r
            if unroll % nb == 0:
                first = u % nb == 0
                kstart = r if first else start - j * dil
                sel = 0 if first else 1
            elif u == 0:
                first = b == 0
                kstart = jnp.where(first, r, start - j * dil)
                sel = jnp.where(first, 0, 1)
            else:
                kstart = start - j * dil
                sel = 1
            if dil == 1:
                start = pl.multiple_of(start, j)
                kstart = pl.multiple_of(kstart, j)
            q = q_ref[rows(start, j), :] * (HEAD_DIM ** -0.5)
            kk = k_ref[rows(kstart, 2 * j), :].astype(BF16)
            vv = v_ref[rows(kstart, 2 * j), :].astype(BF16)
            old = None
            if not init:
                old = (m_s[0, rows(start, j), :], m_s[1, rows(start, j), :],
                       l_s[0, rows(start, j), :], l_s[1, rows(start, j), :],
                       acc_s[rows(start, j), :])
            return start, sel, q, kk, vv, old

        def compute(sel, q, kk, vv, old):
            vx = jnp.concatenate([vv, ones], axis=1)
            m_out, l_out, acc_new = [], [], None
            for half in (0, 1):
                hmask = half1 if half else jnp.logical_not(half1)
                qh = jnp.where(hmask, q, 0.0).astype(BF16)
                s = _dot_nt(qh, kk) + bias_s[g, half, sel]
                mx = jnp.max(s, axis=-1, keepdims=True)
                if init:
                    m_new = jnp.broadcast_to(mx, (j, LANES))
                else:
                    m_new = jnp.maximum(old[half], mx)
                p = jnp.exp(s - jnp.concatenate([m_new, m_new], axis=1)).astype(BF16)
                pvx = _dot(p, vx)
                pv, rs = pvx[:, :LANES], pvx[:, LANES:]
                if init:
                    l_new, contrib = rs, pv
                else:
                    alpha = jnp.exp(old[half] - m_new)
                    l_new = alpha * old[2 + half] + rs
                    contrib = alpha * old[4] + pv
                m_out.append(m_new)
                l_out.append(l_new)
                acc_new = contrib if half == 0 else jnp.where(half1, contrib, acc_new)
            return m_out, l_out, acc_new

        def body(it, carry):
            loaded = [load(it * unroll + u, u) for u in range(unroll)]
            done = [(ld[0],) + compute(*ld[1:]) for ld in loaded]
            for start, m_out, l_out, acc_new in done:
                for half in (0, 1):
                    m_s[half, rows(start, j), :] = m_out[half]
                    l_s[half, rows(start, j), :] = l_out[half]
                acc_s[rows(start, j), :] = acc_new
            return carry

        lax.fori_loop(0, nb * dil // unroll, body, 0)

    order = sorted(range(len(ATT_DILATIONS)), key=lambda g: -ATT_DILATIONS[g])
    for n, g in enumerate(order):
        branch(g, ATT_DILATIONS[g], n == 0)

    def finish(c, carry):
        rows = pl.ds(pl.multiple_of(c * j, j), j)
        l = jnp.where(half1, l_s[1, rows, :], l_s[0, rows, :])
        o_ref[rows, :] = acc_s[rows, :] / l
        return carry

    lax.fori_loop(0, seq // j, finish, 0)


def _attn_prompt(h, slopes, n_seq, seq):
    npairs = ATT_W // LANES
    blk = (seq, LANES)
    est = 5 * 2 * seq * LANES * 4 + 5 * seq * LANES * 4 + (8 << 20)
    return pl.pallas_call(
        functools.partial(_attn_prompt_kernel, seq=seq),
        grid=(n_seq, npairs),
        in_specs=[
            pl.BlockSpec(memory_space=pltpu.SMEM),
            pl.BlockSpec(blk, lambda n, p: (n, p)),
            pl.BlockSpec(blk, lambda n, p: (n, npairs + p)),
            pl.BlockSpec(blk, lambda n, p: (n, 2 * npairs + p)),
        ],
        out_specs=pl.BlockSpec(blk, lambda n, p: (n, p)),
        out_shape=jax.ShapeDtypeStruct((n_seq * seq, ATT_W), F32),
        scratch_shapes=[
            pltpu.VMEM((2, seq, LANES), F32),
            pltpu.VMEM((2, seq, LANES), F32),
            pltpu.VMEM((seq, LANES), F32),
            pltpu.VMEM((len(ATT_DILATIONS), 2, 2, ATT_J, 2 * ATT_J), F32),
        ],
        compiler_params=pltpu.CompilerParams(
            dimension_semantics=("arbitrary", "arbitrary"), vmem_limit_bytes=_vmem_limit(est)),
        name="attn_prompt",
    )(slopes, h, h, h)


def _attn_decode_tables(t_new, win):
    cols = win + LANES
    c = np.arange(cols)[None, :]
    i = np.arange(t_new)[:, None]
    delta = win + i - c
    ok = (delta >= 0) & (c < win + t_new)
    mult = np.zeros((t_new, cols), np.float32)
    for dil in ATT_DILATIONS:
        mult += (ok & (delta % dil == 0) & (delta <= ATT_J * dil)).astype(np.float32)
    slopes = 2.0 ** (-8.0 * np.arange(1, N_HEADS_ATT + 1) / N_HEADS_ATT)
    bias = -slopes[:, None, None] * delta[None].astype(np.float64)
    bias = np.where(mult[None] > 0, bias, -NEG_BIG).astype(np.float32)
    bias = bias.reshape(N_HEADS_ATT * t_new, cols)
    mult = np.tile(mult, (N_HEADS_ATT, 1))
    return bias, mult


def _attn_decode_kernel(bias_ref, mult_ref, q_ref, k_ref, v_ref, ck_ref, cv_ref,
                        o_ref, wk_ref, wv_ref, *, t_new, win):
    for pp in range(ATT_DECODE_PAIRS):
        cols = slice(pp * LANES, (pp + 1) * LANES)
        rows = slice(pp * 2 * t_new, (pp + 1) * 2 * t_new)
        _attn_decode_pair(bias_ref.at[rows, :], mult_ref.at[rows, :], q_ref.at[:, cols], k_ref.at[:, cols],
                          v_ref.at[:, cols], ck_ref.at[cols, :], cv_ref.at[cols, :], o_ref.at[:, cols],
                          wk_ref.at[cols, :], wv_ref.at[cols, :], t_new=t_new, win=win)


def _attn_decode_pair(bias_ref, mult_ref, q_ref, k_ref, v_ref, ck_ref, cv_ref,
                      o_ref, wk_ref, wv_ref, *, t_new, win):
    lane = lax.broadcasted_iota(jnp.int32, (1, LANES), 1)
    half1 = lane >= HEAD_DIM
    q = q_ref[...] * (HEAD_DIM ** -0.5)
    qs = jnp.concatenate([jnp.where(half1, 0.0, q), jnp.where(half1, q, 0.0)], axis=0).astype(BF16)
    pad = jnp.zeros((LANES - t_new, LANES), F32)
    kt_new = jnp.concatenate([k_ref[...], pad], axis=0).T
    vt_new = jnp.concatenate([v_ref[...], pad], axis=0).T
    ck = ck_ref[...]
    cv = cv_ref[...]
    s = jnp.concatenate([_dot(qs, ck.astype(BF16)), _dot(qs, kt_new.astype(BF16))], axis=1)
    s = s + bias_ref[...]
    mx = jnp.max(s, axis=-1, keepdims=True)
    p = mult_ref[...] * jnp.exp(s - mx)
    l = jnp.sum(p, axis=-1, keepdims=True)
    pb = p.astype(BF16)
    o_all = (_dot_nt(pb[:, :win], cv.astype(BF16)) + _dot_nt(pb[:, win:], vt_new.astype(BF16))) / l
    o_ref[...] = jnp.where(half1, o_all[t_new:], o_all[:t_new])
    for c_ref, new, w_ref in ((ck_ref, kt_new, wk_ref), (cv_ref, vt_new, wv_ref)):
        shifted = pltpu.roll(c_ref[...], win - t_new, axis=1)
        tail = jnp.where(lane >= LANES - t_new, pltpu.roll(new, LANES - t_new, axis=1),
                         shifted[:, win - LANES:])
        w_ref[:, 0:win - LANES] = shifted[:, 0:win - LANES]
        w_ref[:, win - LANES:] = tail


def _attn_decode(h, cache_k, cache_v, wk_buf, wv_buf, layer, n_seq, t_new, win):
    bias, mult = _attn_decode_tables(t_new, win)
    pp = ATT_DECODE_PAIRS
    nsteps = ATT_W // (pp * LANES)
    rows = pp * 2 * t_new
    cols = win + LANES
    tab = pl.BlockSpec((rows, cols), lambda n, p: (p, 0))
    tok = lambda c: pl.BlockSpec((t_new, pp * LANES), lambda n, p, c=c: (n, c * nsteps + p))
    cache = pl.BlockSpec((None, None, pp * LANES, win), lambda n, p: (layer, n, p, 0))
    in_specs = [tab, tab, tok(0), tok(1), tok(2), cache, cache]
    args = [jnp.asarray(bias), jnp.asarray(mult), h, h, h, cache_k, cache_v]
    aliases = {}
    if wk_buf is not None:
        in_specs += [pl.BlockSpec(memory_space=pl.ANY), pl.BlockSpec(memory_space=pl.ANY)]
        args += [wk_buf, wv_buf]
        aliases = {7: 1, 8: 2}
    kernel = functools.partial(_attn_decode_kernel, t_new=t_new, win=win)
    if wk_buf is not None:
        kernel = lambda *refs, _k=kernel: _k(*refs[:7], *refs[9:])
    win_shape = jax.ShapeDtypeStruct((N_LAYERS, n_seq, ATT_W, win), F32)
    est = 16 * pp * LANES * win * 4 + (8 << 20)
    return pl.pallas_call(
        kernel,
        grid=(n_seq, nsteps),
        in_specs=in_specs,
        out_specs=[pl.BlockSpec((t_new, pp * LANES), lambda n, p: (n, p)), cache, cache],
        out_shape=[jax.ShapeDtypeStruct((n_seq * t_new, ATT_W), F32), win_shape, win_shape],
        input_output_aliases=aliases,
        compiler_params=pltpu.CompilerParams(
            dimension_semantics=("arbitrary", "arbitrary"), vmem_limit_bytes=_vmem_limit(est)),
        name="attn_decode",
    )(*args)


SCAN_MXU_LEVELS = 3


def _scan_tables(c):
    levels = int(math.log2(c))
    t = np.arange(c)[:, None]
    u = np.arange(c)[None, :]
    mats, masks = [], [np.eye(c, dtype=np.float32)]
    for li in range(levels):
        m = 1 << li
        bnd = (t // (2 * m)) * 2 * m + m - 1
        right = (t % (2 * m)) >= m
        if li < SCAN_MXU_LEVELS:
            mats.append(np.where(right, (u > bnd) & (u <= t), (u > t) & (u <= bnd)))
        masks.append(((t // (2 * m)) == (u // (2 * m))) & right & ((u % (2 * m)) < m))
    small = np.concatenate(mats, axis=0).astype(np.float32)
    tri = (u <= t).astype(np.float32)
    return (jnp.asarray(np.concatenate([small, small], axis=1), BF16),
            jnp.asarray(np.concatenate([tri, tri, tri], axis=1), BF16),
            jnp.asarray(np.stack(masks).astype(np.float32)))


def _scan_exponents(g, small_ref, tri_ref, c):
    levels = int(math.log2(c))
    g1 = g.astype(BF16)
    r1 = g - g1.astype(F32)
    g2 = r1.astype(BF16)
    g3 = (r1 - g2.astype(F32)).astype(BF16)
    small = jnp.exp2(_dot(small_ref[...], jnp.concatenate([g1, g2], axis=0)))
    e = [small[li * c:(li + 1) * c] for li in range(SCAN_MXU_LEVELS)]
    b = _dot(tri_ref[...], jnp.concatenate([g1, g2, g3], axis=0))
    row = lax.broadcasted_iota(jnp.int32, (c, 1), 0)
    for li in range(SCAN_MXU_LEVELS, levels):
        m = 1 << li
        bnd = jnp.concatenate(
            [jnp.broadcast_to(b[blk * 2 * m + m - 1:blk * 2 * m + m], (2 * m, LANES))
             for blk in range(c // (2 * m))], axis=0)
        sign = jnp.where((row & (2 * m - 1)) >= m, 1.0, -1.0)
        e.append(jnp.exp2((b - bnd) * sign))
    return e, jnp.exp2(b), jnp.exp2(b[c - 1:c] - b)


def _scan_scores(qs, kk, e, masks_ref, c, qmasks=None):
    levels = int(math.log2(c))
    row = lax.broadcasted_iota(jnp.int32, (c, 1), 0)
    reps = 1 if qmasks is None else len(qmasks)

    def lhs(x):
        if qmasks is None:
            return x.astype(BF16)
        return jnp.concatenate([jnp.where(m, x, 0.0) for m in qmasks], axis=0).astype(BF16)

    def mask(i):
        m = masks_ref[i]
        return m if reps == 1 else jnp.concatenate([m] * reps, axis=0)

    a = mask(0) * _dot_nt(lhs(qs), kk.astype(BF16))
    for li in range(levels):
        m = 1 << li
        right = (row & (2 * m - 1)) >= m
        w = jnp.where(right, qs, kk) * e[li]
        a = a + mask(li + 1) * _dot_nt(lhs(w), w.astype(BF16))
    return a


def _pad_rows(x, c):
    if x.shape[0] == c:
        return x
    return jnp.concatenate([x, jnp.zeros((c - x.shape[0], x.shape[1]), x.dtype)], axis=0)


def _gated_head_norm(o, gate, g):
    lane = lax.broadcasted_iota(jnp.int32, (1, LANES), 1)
    half1 = lane >= LIN_V
    o2 = o * o
    parts = []
    for p in range(LIN_VW // LANES):
        o2p = o2[:, p * LANES:(p + 1) * LANES]
        s_all = jnp.sum(o2p, axis=-1, keepdims=True)
        s_hi = jnp.sum(jnp.where(half1, o2p, 0.0), axis=-1, keepdims=True)
        parts.append(jnp.where(half1, s_hi, s_all - s_hi))
    ms = jnp.concatenate(parts, axis=1) * (1.0 / LIN_V)
    on = o * lax.rsqrt(ms + RMS_EPS)
    return on * g * (gate * (1.0 / (1.0 + jnp.exp(-gate))))


def _hgrn_kernel(q_ref, f_ref, i_ref, og_ref, lbl_ref, gh_ref, s0_ref, small_ref, tri_ref, masks_ref,
                 o_ref, sout_ref, st_s, *, layer, c, c_in, cps, n_steps):
    si = pl.program_id(1)

    @pl.when(si == 0)
    def _():
        st_s[...] = s0_ref[0]

    lbl = lbl_ref[...]
    ex = jnp.exp(lbl - jnp.max(lbl, axis=0, keepdims=True))
    prob = ex / jnp.sum(ex, axis=0, keepdims=True)
    lb = jnp.zeros((1, HGRN_KW), F32)
    for r in range(1, layer + 1):
        lb = lb + prob[r:r + 1]
    lane = lax.broadcasted_iota(jnp.int32, (1, LANES), 1)
    half1 = lane >= LIN_V

    def chunk(rows):
        q = _pad_rows(q_ref[rows, :], c)
        f = _pad_rows(f_ref[rows, :], c)
        vi = _pad_rows(i_ref[rows, :], c)
        forget = lb + (1.0 - lb) * (1.0 / (1.0 + jnp.exp(-f)))
        g = jnp.log2(forget)
        k = 1.0 - forget
        if c_in < c:
            valid = lax.broadcasted_iota(jnp.int32, (c, 1), 0) < c_in
            g = jnp.where(valid, g, 0.0)
            k = jnp.where(valid, k, 0.0)
        pairs = []
        for p in range(N_HEADS_LIN // 2):
            v = vi[:, p * LANES:(p + 1) * LANES]
            vb = v.astype(BF16)
            vt = v.T.astype(BF16)
            scores, inter, decay, kus = [], [], [], []
            for half in range(2):
                hd = 2 * p + half
                cols = slice(hd * HGRN_K, (hd + 1) * HGRN_K)
                qs = q[:, cols] * (HGRN_K ** -0.5)
                kk = k[:, cols]
                e, eb, eu = _scan_exponents(g[:, cols], small_ref, tri_ref, c)
                scores.append(_scan_scores(qs, kk, e, masks_ref, c).astype(BF16))
                inter.append(_dot_nt((qs * eb).astype(BF16), st_s[hd].astype(BF16)))
                decay.append(eb[c - 1:c, :])
                kus.append((kk * eu).astype(BF16))
            av = _dot(jnp.concatenate(scores, axis=0), vb)
            upd = _dot(vt, jnp.concatenate(kus, axis=1))
            for half in range(2):
                hd = 2 * p + half
                st_s[hd] = st_s[hd] * decay[half] + upd[:, half * HGRN_K:(half + 1) * HGRN_K]
            o0 = av[:c] + inter[0]
            o1 = av[c:] + inter[1]
            pairs.append(jnp.where(half1, o1, o0))
        o = jnp.concatenate(pairs, axis=1)
        out = _gated_head_norm(o, _pad_rows(og_ref[rows, :], c), gh_ref[...])
        o_ref[rows, :] = out[:c_in]

    _for_chunks(chunk, cps, c_in)

    @pl.when(si == n_steps - 1)
    def _():
        sout_ref[0] = st_s[...]


def _for_chunks(chunk, cps, c_in):
    if cps == 1:
        chunk(slice(None))
    else:
        def body(ci, carry):
            chunk(pl.ds(pl.multiple_of(ci * c_in, c_in), c_in))
            return carry
        lax.fori_loop(0, cps, body, 0)


def _gla_kernel(q_ref, k_ref, v_ref, r_ref, og_ref, wg_ref, bg_ref, gg_ref, s0_ref, bd_ref,
                small_ref, tri_ref, masks_ref, o_ref, sout_ref, st_s, *, c, c_in, cps, n_steps):
    si = pl.program_id(1)

    @pl.when(si == 0)
    def _():
        st_s[...] = s0_ref[0]

    klane = lax.broadcasted_iota(jnp.int32, (1, GLA_KW), 1) // GLA_K
    vlane = lax.broadcasted_iota(jnp.int32, (1, LIN_VW), 1) // LIN_V

    def chunk(rows):
        qs = _pad_rows(q_ref[rows, :], c) * (GLA_K ** -0.5)
        kk = _pad_rows(k_ref[rows, :], c)
        v = _pad_rows(v_ref[rows, :], c)
        x = _dot(_pad_rows(r_ref[rows, :], c).astype(BF16), wg_ref[...]) + bg_ref[...]
        g = (jnp.minimum(x, 0.0) - jnp.log(1.0 + jnp.exp(-jnp.abs(x)))) * (LOG2E / GLA_GATE_NORM)
        if c_in < c:
            valid = lax.broadcasted_iota(jnp.int32, (c, 1), 0) < c_in
            g = jnp.where(valid, g, 0.0)
            kk = jnp.where(valid, kk, 0.0)
        e, eb, eu = _scan_exponents(g, small_ref, tri_ref, c)
        st = st_s[...]
        vb = v.astype(BF16)
        o = _dot_nt((qs * eb).astype(BF16), st.astype(BF16))
        a = _scan_scores(qs, kk, e, masks_ref, c, qmasks=[klane == hd for hd in range(N_HEADS_LIN)])
        av = _dot(a.astype(BF16), vb)
        for hd in range(N_HEADS_LIN):
            o = o + jnp.where(vlane == hd, av[hd * c:(hd + 1) * c], 0.0)
        upd = _dot(v.T.astype(BF16), (kk * eu).astype(BF16))
        st_s[...] = (st * eb[c - 1:c, :] + upd) * bd_ref[...]
        out = _gated_head_norm(o, _pad_rows(og_ref[rows, :], c), gg_ref[...])
        o_ref[rows, :] = out[:c_in]

    _for_chunks(chunk, cps, c_in)

    @pl.when(si == n_steps - 1)
    def _():
        sout_ref[0] = st_s[...]


def _scan_call(kernel_fn, name, h, tok_cols, consts_before, s0k, consts_after, n_seq, seq, **static):
    c = SCAN_C
    c_in = min(seq, c)
    cps = max(min(seq, SCAN_ROWS_PER_STEP) // c, 1)
    n_steps = seq // (c_in * cps)
    rows = c_in * cps
    tables = _scan_tables(c)
    tok = [pl.BlockSpec((rows, w), lambda n, si, col=col: (n * n_steps + si, col))
           for (w, col) in tok_cols]
    cst = lambda x: pl.BlockSpec(x.shape, lambda n, si, nd=x.ndim: (0,) * nd)
    state_shape = s0k.shape[1:]
    state = pl.BlockSpec((1,) + state_shape, lambda n, si: (n,) + (0,) * len(state_shape))
    consts_after = list(consts_after) + list(tables)
    return pl.pallas_call(
        functools.partial(kernel_fn, c=c, c_in=c_in, cps=cps, n_steps=n_steps, **static),
        grid=(n_seq, n_steps),
        in_specs=tok + [cst(x) for x in consts_before] + [state] + [cst(x) for x in consts_after],
        out_specs=[pl.BlockSpec((rows, LIN_VW), lambda n, si: (n * n_steps + si, 0)), state],
        out_shape=[jax.ShapeDtypeStruct((n_seq * seq, LIN_VW), F32),
                   jax.ShapeDtypeStruct((n_seq,) + state_shape, F32)],
        scratch_shapes=[pltpu.VMEM(state_shape, F32)],
        compiler_params=pltpu.CompilerParams(
            dimension_semantics=("arbitrary", "arbitrary"),
            vmem_limit_bytes=_vmem_limit(48 << 20)),
        name=name,
    )(*([h] * len(tok_cols)), *consts_before, s0k, *consts_after)


def _hgrn_scan(h, lb_logits, g_hgrn, s0, layer, n_seq, seq):
    s0t = jnp.swapaxes(s0, 2, 3)
    zero = jnp.zeros_like(s0t)
    even = jnp.concatenate([s0t, zero], axis=2)
    odd = jnp.concatenate([zero, s0t], axis=2)
    is_odd = (jnp.arange(N_HEADS_LIN) % 2 == 1)[None, :, None, None]
    s0k = jnp.where(is_odd, odd, even)
    tok_cols = [(HGRN_KW, 3), (HGRN_KW, 4), (LIN_VW, 10), (LIN_VW, 11)]
    o, sk = _scan_call(_hgrn_kernel, "hgrn_scan", h, tok_cols, [lb_logits, g_hgrn], s0k, [],
                       n_seq, seq, layer=layer)
    sk = sk.reshape(n_seq, N_HEADS_LIN, 2, LIN_V, HGRN_K)
    s_even, s_odd = sk[:, :, 0], sk[:, :, 1]
    st = jnp.where(is_odd, s_odd, s_even)
    return o, jnp.swapaxes(st, 2, 3)


def _gla_scan(h, wg, bg, g_gla, s0, n_seq, seq):
    bd = (np.arange(LIN_VW)[:, None] // LIN_V == np.arange(GLA_KW)[None, :] // GLA_K)
    bd = jnp.asarray(bd.astype(np.float32))
    s0t = jnp.swapaxes(s0, 2, 3)
    eye = jnp.eye(N_HEADS_LIN, dtype=F32)[None, :, None, :, None]
    s0k = (s0t[:, :, :, None, :] * eye).reshape(n_seq, LIN_VW, GLA_KW)
    tok_cols = [(GLA_KW, 24), (GLA_KW, 25), (LIN_VW, 13), (LANES, 30), (LIN_VW, 14)]
    o, sk = _scan_call(_gla_kernel, "gla_scan", h, tok_cols, [wg, bg, g_gla], s0k, [bd], n_seq, seq)
    st = jnp.stack([sk[:, hd * LIN_V:(hd + 1) * LIN_V, hd * GLA_K:(hd + 1) * GLA_K]
                    for hd in range(N_HEADS_LIN)], axis=1)
    return o, jnp.swapaxes(st, 2, 3)


def _gelu_tanh(x):
    return x * (0.5 * (1.0 + jnp.tanh(math.sqrt(2.0 / math.pi) * (x + 0.044715 * (x * x * x)))))


FFN_CHUNK = 256
FFN_SLABS = 8


def _post_kernel(*refs, tm, seq_rows, tiles_per_seq):
    per_tile_seqs = seq_rows < tm
    if per_tile_seqs:
        (oa_ref, ob_ref, oc_ref, x_ref, wout_ref, gpm_ref, gpf_ref, wup_ref, wconv_ref,
         wdown_ref, gpo_ref, s1_ref, s2_ref, xo_ref, u_ref, slab_s, halo_s) = refs
    else:
        (oa_ref, ob_ref, oc_ref, x_ref, wout_ref, gpm_ref, gpf_ref, wup_ref, wconv_ref,
         wdown_ref, gpo_ref, prev_ref, xo_ref, conv_ref, slab_s, halo_s) = refs
    i = pl.program_id(0)
    w2 = 2 * D_FF
    h0 = SUBLANES
    mix = jnp.concatenate([oa_ref[...], ob_ref[...], oc_ref[...]], axis=1).astype(BF16)
    x1 = x_ref[...] + _rms(_dot(mix, wout_ref[...]), gpm_ref[...])
    xn = _rms(x1, gpf_ref[...]).astype(BF16)
    if per_tile_seqs:
        halo_s[...] = jnp.zeros((h0, w2), F32)
        t = lax.broadcasted_iota(jnp.int32, (tm, 1), 0) % seq_rows
    else:
        @pl.when(i % tiles_per_seq == 0)
        def _():
            halo_s[0:h0 - 2, :] = jnp.zeros((h0 - 2, w2), F32)
            halo_s[h0 - 2:h0, :] = prev_ref[0]

    def conv_tile(u_tile, col, ti):
        cols = slice(col, col + LANES)
        slab_s[ti, 0:h0, :] = halo_s[:, cols]
        slab_s[ti, h0:h0 + tm, :] = u_tile
        sh1 = slab_s[ti, h0 - 1:h0 - 1 + tm, :]
        sh2 = slab_s[ti, h0 - 2:h0 - 2 + tm, :]
        if per_tile_seqs:
            sh1 = jnp.where(t >= 1, sh1, s1_ref[:, cols])
            sh2 = jnp.where(t >= 2, sh2, s2_ref[:, cols])
            u_ref[:, cols] = u_tile
        else:
            conv_ref[0, :, cols] = u_tile[tm - 2:tm]
            halo_s[:, cols] = u_tile[tm - h0:tm]
        return sh2 * wconv_ref[0:1, cols] + sh1 * wconv_ref[1:2, cols] + u_tile * wconv_ref[2:3, cols]

    ffn = jnp.zeros((tm, D_MODEL), F32)
    tiles = FFN_CHUNK // LANES
    for ci, c0 in enumerate(range(0, D_FF, FFN_CHUNK)):
        parts = []
        for pi, base in enumerate((c0, D_FF + c0)):
            u = _dot(xn, wup_ref[:, base:base + FFN_CHUNK])
            slab0 = ((ci % 2) * 2 + pi) * tiles
            parts.append(jnp.concatenate(
                [conv_tile(u[:, k * LANES:(k + 1) * LANES], base + k * LANES, slab0 + k)
                 for k in range(tiles)], axis=1))
        act = (_gelu_tanh(parts[0]) * parts[1]).astype(BF16)
        ffn = ffn + _dot(act, wdown_ref[c0:c0 + FFN_CHUNK, :])
    xo_ref[...] = x1 + _rms(ffn, gpo_ref[...])


def _post_mix_ffn(oa, ob, oc, x, wout, gpm, gpf, wup, wconv, wdown, gpo, conv_prev, n_seq, seq, tm):
    m = x.shape[0]
    w2 = 2 * D_FF
    per_tile_seqs = seq < tm
    tiles_per_seq = max(seq // tm, 1)
    row = lambda w: pl.BlockSpec((tm, w), lambda i: (i, 0))
    cst = lambda shape: pl.BlockSpec(shape, lambda i, nd=len(shape): (0,) * nd,
                                     pipeline_mode=pl.Buffered(1))
    in_specs = [row(ATT_W), row(LIN_VW), row(LIN_VW), row(D_MODEL),
                cst((D_MODEL, D_MODEL)), cst((1, D_MODEL)), cst((1, D_MODEL)),
                cst((D_MODEL, w2)), cst((CONV_W, w2)), cst((D_FF, D_MODEL)), cst((1, D_MODEL))]
    args = [oa, ob, oc, x, wout, gpm, gpf, wup, wconv, wdown, gpo]
    if per_tile_seqs:
        s1 = jnp.pad(conv_prev[:, 1:2], ((0, 0), (0, seq - 1), (0, 0)))
        s2 = jnp.pad(conv_prev, ((0, 0), (0, seq - 2), (0, 0)))
        in_specs += [row(w2), row(w2)]
        args += [s1.reshape(m, w2), s2.reshape(m, w2)]
        second = pl.BlockSpec((tm, w2), lambda i: (i, 0))
        second_shape = jax.ShapeDtypeStruct((m, w2), F32)
    else:
        in_specs += [pl.BlockSpec((1, CONV_W - 1, w2), lambda i: (i // tiles_per_seq, 0, 0))]
        args += [conv_prev]
        second = pl.BlockSpec((1, CONV_W - 1, w2), lambda i: (i // tiles_per_seq, 0, 0))
        second_shape = jax.ShapeDtypeStruct((n_seq, CONV_W - 1, w2), F32)
    est = ((D_MODEL * D_MODEL + D_MODEL * w2 + D_FF * D_MODEL) * 2
           + 6 * tm * w2 * 4 + 8 * tm * D_MODEL * 4 + (6 << 20))
    return pl.pallas_call(
        functools.partial(_post_kernel, tm=tm, seq_rows=seq, tiles_per_seq=tiles_per_seq),
        grid=(m // tm,),
        in_specs=in_specs,
        out_specs=[row(D_MODEL), second],
        out_shape=[jax.ShapeDtypeStruct((m, D_MODEL), F32), second_shape],
        scratch_shapes=[pltpu.VMEM((FFN_SLABS, tm + SUBLANES, LANES), F32),
                        pltpu.VMEM((SUBLANES, w2), F32)],
        compiler_params=pltpu.CompilerParams(
            dimension_semantics=("arbitrary",), vmem_limit_bytes=_vmem_limit(est)),
        name="post_mix_ffn",
    )(*args)


def _permute_w_in(w):
    r0 = IN_COLS - LIN_VW - GLA_RANK
    pad = jnp.zeros((w.shape[0], IN_COLS_PAD - IN_COLS), w.dtype)
    return jnp.concatenate([w[:, :r0], w[:, r0 + GLA_RANK:], w[:, r0:r0 + GLA_RANK], pad],
                           axis=1).astype(BF16)


def kernel(x_prompt, x_sample, cache_win_k, cache_win_v, state_hgrn, state_gla, state_conv,
           w_in, w_gate_up, b_gate_up, w_out, g_hgrn, g_gla, lb_logits,
           w_up, w_conv, w_down, g_pre_mix, g_post_mix, g_pre_ffn, g_post_ffn):
    nb, seq, d = x_prompt.shape
    db, t_new, _ = x_sample.shape
    win = cache_win_k.shape[2]
    tm = 256
    slopes = jnp.asarray(2.0 ** (-8.0 * np.arange(1, N_HEADS_ATT + 1) / N_HEADS_ATT), F32)
    ck = jnp.transpose(cache_win_k, (0, 1, 3, 4, 2)).reshape(N_LAYERS, db, ATT_W, win)
    cv = jnp.transpose(cache_win_v, (0, 1, 3, 4, 2)).reshape(N_LAYERS, db, ATT_W, win)

    xp = x_prompt.reshape(nb * seq, d)
    xs = x_sample.reshape(db * t_new, d)
    zero_hgrn = jnp.zeros((nb, N_HEADS_LIN, HGRN_K, LIN_V), F32)
    zero_gla = jnp.zeros((nb, N_HEADS_LIN, GLA_K, LIN_V), F32)
    zero_conv = jnp.zeros((nb, CONV_W - 1, 2 * D_FF), F32)
    keep = min(WIN_MAX, seq)

    wk_buf = wv_buf = None
    wkp, wvp, hg_p, hg_s, gl_p, gl_s, cv_p, cv_s = [], [], [], [], [], [], [], []
    for l in range(N_LAYERS):
        w_in_l = _permute_w_in(w_in[l])
        wg = jnp.zeros((LANES, GLA_KW), F32).at[:GLA_RANK].set(w_gate_up[l]).astype(BF16)
        bg = b_gate_up[l].reshape(1, GLA_KW)
        gpre = g_pre_mix[l].reshape(1, d)
        gh = g_hgrn[l].reshape(1, LIN_VW)
        gg = g_gla[l].reshape(1, LIN_VW)
        post_w = (w_out[l].astype(BF16), g_post_mix[l].reshape(1, d), g_pre_ffn[l].reshape(1, d),
                  w_up[l].astype(BF16), w_conv[l], w_down[l].astype(BF16), g_post_ffn[l].reshape(1, d))

        hp = _norm_matmul(xp, gpre, w_in_l, tm)
        oa = _attn_prompt(hp, slopes, nb, seq)
        ob, sh = _hgrn_scan(hp, lb_logits, gh, zero_hgrn, l, nb, seq)
        oc, sg = _gla_scan(hp, wg, bg, gg, zero_gla, nb, seq)
        xp, conv_p = _post_mix_ffn(oa, ob, oc, xp, *post_w, zero_conv, nb, seq, 2 * tm)
        h3 = hp.reshape(nb, seq, IN_COLS_PAD)
        wkp.append(h3[:, seq - keep:, ATT_W:2 * ATT_W])
        wvp.append(h3[:, seq - keep:, 2 * ATT_W:3 * ATT_W])
        hg_p.append(sh)
        gl_p.append(sg)
        cv_p.append(conv_p)

        hs = _norm_matmul(xs, gpre, w_in_l, db * t_new)
        oa, wk_buf, wv_buf = _attn_decode(hs, ck, cv, wk_buf, wv_buf, l, db, t_new, win)
        ob, sh = _hgrn_scan(hs, lb_logits, gh, state_hgrn[l], l, db, t_new)
        oc, sg = _gla_scan(hs, wg, bg, gg, state_gla[l], db, t_new)
        xs, u_s = _post_mix_ffn(oa, ob, oc, xs, *post_w, state_conv[l], db, t_new, db * t_new)
        hg_s.append(sh)
        gl_s.append(sg)
        cv_s.append(u_s.reshape(db, t_new, 2 * D_FF)[:, t_new - (CONV_W - 1):])

    hshape = (N_LAYERS, nb, keep, N_HEADS_ATT, HEAD_DIM)

    def token_major(w):
        w = w.reshape(N_LAYERS, db, N_HEADS_ATT, HEAD_DIM, win)
        return jnp.transpose(w, (0, 1, 4, 2, 3))
    return (xp.reshape(nb, seq, d), xs.reshape(db, t_new, d),
            jnp.stack(wkp).reshape(hshape), jnp.stack(wvp).reshape(hshape),
            token_major(wk_buf), token_major(wv_buf),
            jnp.stack(hg_p), jnp.stack(hg_s), jnp.stack(gl_p), jnp.stack(gl_s),
            jnp.stack(cv_p), jnp.stack(cv_s))
```

```python
import functools
import math

import numpy as np
import jax
import jax.numpy as jnp
from jax import lax
from jax.experimental import pallas as pl
from jax.experimental.pallas import tpu as pltpu

F32 = jnp.float32
BF16 = jnp.bfloat16

D_MODEL = 1024
N_LAYERS = 4
HEAD_DIM = 64
N_HEADS_ATT = 8
ATT_W = N_HEADS_ATT * HEAD_DIM
ATT_J = 128
ATT_DILATIONS = (1, 4, 16)
WIN_MAX = 2048
ATT_UNROLL = 8
ATT_DECODE_PAIRS = 2
N_HEADS_LIN = 4
HGRN_K = 128
GLA_K = 32
LIN_V = 64
LIN_VW = N_HEADS_LIN * LIN_V
HGRN_KW = N_HEADS_LIN * HGRN_K
GLA_KW = N_HEADS_LIN * GLA_K
GLA_RANK = 16
GLA_GATE_NORM = 16.0
D_FF = 2816
CONV_W = 3
RMS_EPS = 1e-6
IN_COLS = 3856
IN_COLS_PAD = 3968

LANES = 128
SUBLANES = 8
VMEM_LIMIT_CAP = 56 * 1024 * 1024

SCAN_C = 128
NEG_BIG = 1e30
LOG2E = 1.4426950408889634
SCAN_ROWS_PER_STEP = 512


def _vmem_limit(nbytes):
    return int(min(VMEM_LIMIT_CAP, max(32 * 1024 * 1024, nbytes)))


def _rms(x, g):
    ms = jnp.mean(x * x, axis=-1, keepdims=True)
    return x * lax.rsqrt(ms + RMS_EPS) * g


def _dot(a, b):
    return jnp.dot(a, b, preferred_element_type=F32)


def _dot_nt(a, b):
    return lax.dot_general(a, b, (((1,), (1,)), ((), ())), preferred_element_type=F32)


def _norm_matmul_kernel(x_ref, g_ref, w_ref, o_ref):
    xn = _rms(x_ref[...], g_ref[...]).astype(BF16)
    o_ref[...] = _dot(xn, w_ref[...])


def _norm_matmul(x, g, w, tm):
    m, d = x.shape
    n = w.shape[1]
    est = 2 * tm * d * 4 + 2 * d * n * 2 + 2 * tm * n * 4 + (4 << 20)
    return pl.pallas_call(
        _norm_matmul_kernel,
        grid=(m // tm,),
        in_specs=[
            pl.BlockSpec((tm, d), lambda i: (i, 0)),
            pl.BlockSpec((1, d), lambda i: (0, 0)),
            pl.BlockSpec((d, n), lambda i: (0, 0)),
        ],
        out_specs=pl.BlockSpec((tm, n), lambda i: (i, 0)),
        out_shape=jax.ShapeDtypeStruct((m, n), F32),
        compiler_params=pltpu.CompilerParams(
            dimension_semantics=("arbitrary",), vmem_limit_bytes=_vmem_limit(est)),
        name="norm_matmul",
    )(x, g, w)


def _attn_prompt_kernel(slopes_ref, q_ref, k_ref, v_ref, o_ref, kw_ref, vw_ref, m_s, l_s, acc_s, bias_s,
                        *, seq, keep):
    hp = pl.program_id(1)
    kw_ref[...] = k_ref[seq - keep:seq, :]
    vw_ref[...] = v_ref[seq - keep:seq, :]
    j = ATT_J
    qi = lax.broadcasted_iota(jnp.int32, (j, 2 * j), 0)
    ki = lax.broadcasted_iota(jnp.int32, (j, 2 * j), 1)
    d1 = j + qi - ki
    dt1 = jnp.where((d1 >= 0) & (d1 <= j), d1.astype(F32), NEG_BIG)
    d0 = qi - ki
    dt0 = jnp.where((ki < j) & (d0 >= 0), d0.astype(F32), NEG_BIG)
    for g, dil in enumerate(ATT_DILATIONS):
        for half in (0, 1):
            sc = slopes_ref[2 * hp + half] * float(dil)
            bias_s[g, half, 0] = -sc * dt0
            bias_s[g, half, 1] = -sc * dt1
    lane = lax.broadcasted_iota(jnp.int32, (1, LANES), 1)
    half1 = lane >= HEAD_DIM
    ones = jnp.ones((2 * j, LANES), BF16)

    def branch(g, dil, init):
        nb = seq // (j * dil)
        lognb = int(math.log2(nb))
        unroll = ATT_UNROLL
        assert unroll % nb == 0 or nb % unroll == 0

        def rows(start, size):
            return pl.ds(start, size) if dil == 1 else pl.ds(start, size, stride=dil)

        def load(idx, u):
            b = idx & (nb - 1)
            r = idx >> lognb
            start = b * (j * dil) + r
            if unroll % nb == 0:
                first = u % nb == 0
                kstart = r if first else start - j * dil
                sel = 0 if first else 1
            elif u == 0:
                first = b == 0
                kstart = jnp.where(first, r, start - j * dil)
                sel = jnp.where(first, 0, 1)
            else:
                kstart = start - j * dil
                sel = 1
            if dil == 1:
                start = pl.multiple_of(start, j)
                kstart = pl.multiple_of(kstart, j)
            q = q_ref[rows(start, j), :] * (HEAD_DIM ** -0.5)
            kk = k_ref[rows(kstart, 2 * j), :].astype(BF16)
            vv = v_ref[rows(kstart, 2 * j), :].astype(BF16)
            old = None
            if not init:
                old = (m_s[0, rows(start, j), :], m_s[1, rows(start, j), :],
                       l_s[0, rows(start, j), :], l_s[1, rows(start, j), :],
                       acc_s[rows(start, j), :])
            return start, sel, q, kk, vv, old

        def compute(sel, q, kk, vv, old):
            vx = jnp.concatenate([vv, ones], axis=1)
            m_out, l_out, acc_new = [], [], None
            for half in (0, 1):
                hmask = half1 if half else jnp.logical_not(half1)
                qh = jnp.where(hmask, q, 0.0).astype(BF16)
                s = _dot_nt(qh, kk) + bias_s[g, half, sel]
                mx = jnp.max(s, axis=-1, keepdims=True)
                if init:
                    m_new = jnp.broadcast_to(mx, (j, LANES))
                else:
                    m_new = jnp.maximum(old[half], mx)
                p = jnp.exp(s - jnp.concatenate([m_new, m_new], axis=1)).astype(BF16)
                pvx = _dot(p, vx)
                pv, rs = pvx[:, :LANES], pvx[:, LANES:]
                if init:
                    l_new, contrib = rs, pv
                else:
                    alpha = jnp.exp(old[half] - m_new)
                    l_new = alpha * old[2 + half] + rs
                    contrib = alpha * old[4] + pv
                m_out.append(m_new)
                l_out.append(l_new)
                acc_new = contrib if half == 0 else jnp.where(half1, contrib, acc_new)
            return m_out, l_out, acc_new

        def body(it, carry):
            loaded = [load(it * unroll + u, u) for u in range(unroll)]
            done = [(ld[0],) + compute(*ld[1:]) for ld in loaded]
            for start, m_out, l_out, acc_new in done:
                for half in (0, 1):
                    m_s[half, rows(start, j), :] = m_out[half]
                    l_s[half, rows(start, j), :] = l_out[half]
                acc_s[rows(start, j), :] = acc_new
            return carry

        lax.fori_loop(0, nb * dil // unroll, body, 0)

    order = sorted(range(len(ATT_DILATIONS)), key=lambda g: -ATT_DILATIONS[g])
    for n, g in enumerate(order):
        branch(g, ATT_DILATIONS[g], n == 0)

    def finish(c, carry):
        rows = pl.ds(pl.multiple_of(c * j, j), j)
        l = jnp.where(half1, l_s[1, rows, :], l_s[0, rows, :])
        o_ref[rows, :] = acc_s[rows, :] / l
        return carry

    lax.fori_loop(0, seq // j, finish, 0)


def _attn_prompt(h, slopes, n_seq, seq, keep):
    npairs = ATT_W // LANES
    blk = (seq, LANES)
    wblk = pl.BlockSpec((keep, LANES), lambda n, p: (n, p))
    wshape = jax.ShapeDtypeStruct((n_seq * keep, ATT_W), F32)
    est = 5 * 2 * seq * LANES * 4 + 5 * seq * LANES * 4 + (8 << 20)
    return pl.pallas_call(
        functools.partial(_attn_prompt_kernel, seq=seq, keep=keep),
        grid=(n_seq, npairs),
        in_specs=[
            pl.BlockSpec(memory_space=pltpu.SMEM),
            pl.BlockSpec(blk, lambda n, p: (n, p)),
            pl.BlockSpec(blk, lambda n, p: (n, npairs + p)),
            pl.BlockSpec(blk, lambda n, p: (n, 2 * npairs + p)),
        ],
        out_specs=[pl.BlockSpec(blk, lambda n, p: (n, p)), wblk, wblk],
        out_shape=[jax.ShapeDtypeStruct((n_seq * seq, ATT_W), F32), wshape, wshape],
        scratch_shapes=[
            pltpu.VMEM((2, seq, LANES), F32),
            pltpu.VMEM((2, seq, LANES), F32),
            pltpu.VMEM((seq, LANES), F32),
            pltpu.VMEM((len(ATT_DILATIONS), 2, 2, ATT_J, 2 * ATT_J), F32),
        ],
        compiler_params=pltpu.CompilerParams(
            dimension_semantics=("arbitrary", "arbitrary"), vmem_limit_bytes=_vmem_limit(est)),
        name="attn_prompt",
    )(slopes, h, h, h)


def _attn_decode_tables(t_new, win):
    cols = win + LANES
    c = np.arange(cols)[None, :]
    i = np.arange(t_new)[:, None]
    delta = win + i - c
    ok = (delta >= 0) & (c < win + t_new)
    mult = np.zeros((t_new, cols), np.float32)
    for dil in ATT_DILATIONS:
        mult += (ok & (delta % dil == 0) & (delta <= ATT_J * dil)).astype(np.float32)
    slopes = 2.0 ** (-8.0 * np.arange(1, N_HEADS_ATT + 1) / N_HEADS_ATT)
    bias = -slopes[:, None, None] * delta[None].astype(np.float64)
    bias = np.where(mult[None] > 0, bias, -NEG_BIG).astype(np.float32)
    bias = bias.reshape(N_HEADS_ATT * t_new, cols)
    mult = np.tile(mult, (N_HEADS_ATT, 1))
    return bias, mult


def _attn_decode_kernel(bias_ref, mult_ref, q_ref, k_ref, v_ref, ck_ref, cv_ref,
                        o_ref, wk_ref, wv_ref, *, t_new, win):
    for pp in range(ATT_DECODE_PAIRS):
        cols = slice(pp * LANES, (pp + 1) * LANES)
        rows = slice(pp * 2 * t_new, (pp + 1) * 2 * t_new)
        _attn_decode_pair(bias_ref.at[rows, :], mult_ref.at[rows, :], q_ref.at[:, cols], k_ref.at[:, cols],
                          v_ref.at[:, cols], ck_ref.at[cols, :], cv_ref.at[cols, :], o_ref.at[:, cols],
                          wk_ref.at[cols, :], wv_ref.at[cols, :], t_new=t_new, win=win)


def _attn_decode_pair(bias_ref, mult_ref, q_ref, k_ref, v_ref, ck_ref, cv_ref,
                      o_ref, wk_ref, wv_ref, *, t_new, win):
    lane = lax.broadcasted_iota(jnp.int32, (1, LANES), 1)
    half1 = lane >= HEAD_DIM
    q = q_ref[...] * (HEAD_DIM ** -0.5)
    qs = jnp.concatenate([jnp.where(half1, 0.0, q), jnp.where(half1, q, 0.0)], axis=0).astype(BF16)
    pad = jnp.zeros((LANES - t_new, LANES), F32)
    kt_new = jnp.concatenate([k_ref[...], pad], axis=0).T
    vt_new = jnp.concatenate([v_ref[...], pad], axis=0).T
    ck = ck_ref[...]
    cv = cv_ref[...]
    s = jnp.concatenate([_dot(qs, ck.astype(BF16)), _dot(qs, kt_new.astype(BF16))], axis=1)
    s = s + bias_ref[...]
    mx = jnp.max(s, axis=-1, keepdims=True)
    p = mult_ref[...] * jnp.exp(s - mx)
    l = jnp.sum(p, axis=-1, keepdims=True)
    pb = p.astype(BF16)
    o_all = (_dot_nt(pb[:, :win], cv.astype(BF16)) + _dot_nt(pb[:, win:], vt_new.astype(BF16))) / l
    o_ref[...] = jnp.where(half1, o_all[t_new:], o_all[:t_new])
    for c_ref, new, w_ref in ((ck_ref, kt_new, wk_ref), (cv_ref, vt_new, wv_ref)):
        shifted = pltpu.roll(c_ref[...], win - t_new, axis=1)
        tail = jnp.where(lane >= LANES - t_new, pltpu.roll(new, LANES - t_new, axis=1),
                         shifted[:, win - LANES:])
        w_ref[:, 0:win - LANES] = shifted[:, 0:win - LANES]
        w_ref[:, win - LANES:] = tail


def _attn_decode(h, cache_k, cache_v, wk_buf, wv_buf, layer, n_seq, t_new, win):
    bias, mult = _attn_decode_tables(t_new, win)
    pp = ATT_DECODE_PAIRS
    nsteps = ATT_W // (pp * LANES)
    rows = pp * 2 * t_new
    cols = win + LANES
    tab = pl.BlockSpec((rows, cols), lambda n, p: (p, 0))
    tok = lambda c: pl.BlockSpec((t_new, pp * LANES), lambda n, p, c=c: (n, c * nsteps + p))
    cache = pl.BlockSpec((None, None, pp * LANES, win), lambda n, p: (layer, n, p, 0))
    in_specs = [tab, tab, tok(0), tok(1), tok(2), cache, cache]
    args = [jnp.asarray(bias), jnp.asarray(mult), h, h, h, cache_k, cache_v]
    aliases = {}
    if wk_buf is not None:
        in_specs += [pl.BlockSpec(memory_space=pl.ANY), pl.BlockSpec(memory_space=pl.ANY)]
        args += [wk_buf, wv_buf]
        aliases = {7: 1, 8: 2}
    kernel = functools.partial(_attn_decode_kernel, t_new=t_new, win=win)
    if wk_buf is not None:
        kernel = lambda *refs, _k=kernel: _k(*refs[:7], *refs[9:])
    win_shape = jax.ShapeDtypeStruct((N_LAYERS, n_seq, ATT_W, win), F32)
    est = 16 * pp * LANES * win * 4 + (8 << 20)
    return pl.pallas_call(
        kernel,
        grid=(n_seq, nsteps),
        in_specs=in_specs,
        out_specs=[pl.BlockSpec((t_new, pp * LANES), lambda n, p: (n, p)), cache, cache],
        out_shape=[jax.ShapeDtypeStruct((n_seq * t_new, ATT_W), F32), win_shape, win_shape],
        input_output_aliases=aliases,
        compiler_params=pltpu.CompilerParams(
            dimension_semantics=("arbitrary", "arbitrary"), vmem_limit_bytes=_vmem_limit(est)),
        name="attn_decode",
    )(*args)


SCAN_MXU_LEVELS = 3


def _scan_tables(c):
    levels = int(math.log2(c))
    t = np.arange(c)[:, None]
    u = np.arange(c)[None, :]
    mats, masks = [], [np.eye(c, dtype=np.float32)]
    for li in range(levels):
        m = 1 << li
        bnd = (t // (2 * m)) * 2 * m + m - 1
        right = (t % (2 * m)) >= m
        if li < SCAN_MXU_LEVELS:
            mats.append(np.where(right, (u > bnd) & (u <= t), (u > t) & (u <= bnd)))
        masks.append(((t // (2 * m)) == (u // (2 * m))) & right & ((u % (2 * m)) < m))
    small = np.concatenate(mats, axis=0).astype(np.float32)
    tri = (u <= t).astype(np.float32)
    return (jnp.asarray(np.concatenate([small, small], axis=1), BF16),
            jnp.asarray(np.concatenate([tri, tri, tri], axis=1), BF16),
            jnp.asarray(np.stack(masks).astype(np.float32)))


def _scan_exponents(g, small_ref, tri_ref, c, levels):
    g1 = g.astype(BF16)
    r1 = g - g1.astype(F32)
    g2 = r1.astype(BF16)
    g3 = (r1 - g2.astype(F32)).astype(BF16)
    n_small = min(levels, SCAN_MXU_LEVELS)
    small = jnp.exp2(_dot(small_ref[0:n_small * c, :], jnp.concatenate([g1, g2], axis=0)))
    e = [small[li * c:(li + 1) * c] for li in range(n_small)]
    b = _dot(tri_ref[...], jnp.concatenate([g1, g2, g3], axis=0))
    row = lax.broadcasted_iota(jnp.int32, (c, 1), 0)
    for li in range(SCAN_MXU_LEVELS, levels):
        m = 1 << li
        bnd = jnp.concatenate(
            [jnp.broadcast_to(b[blk * 2 * m + m - 1:blk * 2 * m + m], (2 * m, g.shape[1]))
             for blk in range(c // (2 * m))], axis=0)
        sign = jnp.where((row & (2 * m - 1)) >= m, 1.0, -1.0)
        e.append(jnp.exp2((b - bnd) * sign))
    return e, jnp.exp2(b), jnp.exp2(b[c - 1:c] - b)


def _stack_heads(x, qmasks):
    if qmasks is None:
        return x.astype(BF16)
    return jnp.concatenate([jnp.where(m, x, 0.0) for m in qmasks], axis=0).astype(BF16)


def _scan_scores(qs, kk, e, masks_ref, c, qmasks=None):
    row = lax.broadcasted_iota(jnp.int32, (c, 1), 0)
    reps = 1 if qmasks is None else len(qmasks)

    def mask(i):
        m = masks_ref[i]
        return m if reps == 1 else jnp.concatenate([m] * reps, axis=0)

    a = mask(0) * _dot_nt(_stack_heads(qs, qmasks), kk.astype(BF16))
    for li in range(len(e)):
        m = 1 << li
        right = (row & (2 * m - 1)) >= m
        w = jnp.where(right, qs, kk) * e[li]
        a = a + mask(li + 1) * _dot_nt(_stack_heads(w, qmasks), w.astype(BF16))
    return a


def _pad_rows(x, c):
    if x.shape[0] == c:
        return x
    return jnp.concatenate([x, jnp.zeros((c - x.shape[0], x.shape[1]), x.dtype)], axis=0)


def _gated_head_norm(o, gate, g):
    lane = lax.broadcasted_iota(jnp.int32, (1, LANES), 1)
    half1 = lane >= LIN_V
    o2 = o * o
    parts = []
    for p in range(LIN_VW // LANES):
        o2p = o2[:, p * LANES:(p + 1) * LANES]
        s_all = jnp.sum(o2p, axis=-1, keepdims=True)
        s_hi = jnp.sum(jnp.where(half1, o2p, 0.0), axis=-1, keepdims=True)
        parts.append(jnp.where(half1, s_hi, s_all - s_hi))
    ms = jnp.concatenate(parts, axis=1) * (1.0 / LIN_V)
    on = o * lax.rsqrt(ms + RMS_EPS)
    return on * g * (gate * (1.0 / (1.0 + jnp.exp(-gate))))


def _hgrn_kernel(q_ref, f_ref, i_ref, og_ref, lbl_ref, gh_ref, s0_ref, small_ref, tri_ref, masks_ref,
                 o_ref, sout_ref, st_s, *, layer, c, c_in, cps, n_steps):
    si = pl.program_id(1)

    @pl.when(si == 0)
    def _():
        st_s[...] = s0_ref[0]

    lbl = lbl_ref[...]
    ex = jnp.exp(lbl - jnp.max(lbl, axis=0, keepdims=True))
    prob = ex / jnp.sum(ex, axis=0, keepdims=True)
    lb = jnp.zeros((1, HGRN_KW), F32)
    for r in range(1, layer + 1):
        lb = lb + prob[r:r + 1]
    lane = lax.broadcasted_iota(jnp.int32, (1, LANES), 1)
    half1 = lane >= LIN_V
    lane2 = lax.broadcasted_iota(jnp.int32, (1, 2 * HGRN_K), 1)
    pair_masks = [lane2 < HGRN_K, lane2 >= HGRN_K]
    levels = max(int(math.log2(c_in)), 1)

    def chunk(rows):
        q = _pad_rows(q_ref[rows, :], c)
        f = _pad_rows(f_ref[rows, :], c)
        vi = _pad_rows(i_ref[rows, :], c)
        forget = lb + (1.0 - lb) * (1.0 / (1.0 + jnp.exp(-f)))
        g = jnp.log2(forget)
        k = 1.0 - forget
        if c_in < c:
            valid = lax.broadcasted_iota(jnp.int32, (c, 1), 0) < c_in
            g = jnp.where(valid, g, 0.0)
            k = jnp.where(valid, k, 0.0)
        pairs = []
        for p in range(N_HEADS_LIN // 2):
            v = vi[:, p * LANES:(p + 1) * LANES]
            cols = slice(p * 2 * HGRN_K, (p + 1) * 2 * HGRN_K)
            qs = q[:, cols] * (HGRN_K ** -0.5)
            kk = k[:, cols]
            e, eb, eu = _scan_exponents(g[:, cols], small_ref, tri_ref, c, levels)
            a = _scan_scores(qs, kk, e, masks_ref, c, qmasks=pair_masks)
            st = st_s[p]
            o2 = (_dot(a.astype(BF16), v.astype(BF16))
                  + _dot_nt(_stack_heads(qs * eb, pair_masks), st.astype(BF16)))
            st_s[p] = st * eb[c - 1:c, :] + _dot(v.T.astype(BF16), (kk * eu).astype(BF16))
            pairs.append(jnp.where(half1, o2[c:], o2[:c]))
        o = jnp.concatenate(pairs, axis=1)
        out = _gated_head_norm(o, _pad_rows(og_ref[rows, :], c), gh_ref[...])
        o_ref[rows, :] = out[:c_in]

    _for_chunks(chunk, cps, c_in)

    @pl.when(si == n_steps - 1)
    def _():
        sout_ref[0] = st_s[...]


def _for_chunks(chunk, cps, c_in):
    if cps == 1:
        chunk(slice(None))
    else:
        def body(ci, carry):
            chunk(pl.ds(pl.multiple_of(ci * c_in, c_in), c_in))
            return carry
        lax.fori_loop(0, cps, body, 0)


def _gla_kernel(q_ref, k_ref, v_ref, r_ref, og_ref, wg_ref, bg_ref, gg_ref, s0_ref, bd_ref,
                small_ref, tri_ref, masks_ref, o_ref, sout_ref, st_s, *, c, c_in, cps, n_steps):
    si = pl.program_id(1)

    @pl.when(si == 0)
    def _():
        st_s[...] = s0_ref[0]

    klane = lax.broadcasted_iota(jnp.int32, (1, GLA_KW), 1) // GLA_K
    vlane = lax.broadcasted_iota(jnp.int32, (1, LIN_VW), 1) // LIN_V

    def chunk(rows):
        qs = _pad_rows(q_ref[rows, :], c) * (GLA_K ** -0.5)
        kk = _pad_rows(k_ref[rows, :], c)
        v = _pad_rows(v_ref[rows, :], c)
        x = _dot(_pad_rows(r_ref[rows, :], c).astype(BF16), wg_ref[...]) + bg_ref[...]
        g = (jnp.minimum(x, 0.0) - jnp.log(1.0 + jnp.exp(-jnp.abs(x)))) * (LOG2E / GLA_GATE_NORM)
        if c_in < c:
            valid = lax.broadcasted_iota(jnp.int32, (c, 1), 0) < c_in
            g = jnp.where(valid, g, 0.0)
            kk = jnp.where(valid, kk, 0.0)
        e, eb, eu = _scan_exponents(g, small_ref, tri_ref, c, max(int(math.log2(c_in)), 1))
        st = st_s[...]
        vb = v.astype(BF16)
        o = _dot_nt((qs * eb).astype(BF16), st.astype(BF16))
        a = _scan_scores(qs, kk, e, masks_ref, c, qmasks=[klane == hd for hd in range(N_HEADS_LIN)])
        av = _dot(a.astype(BF16), vb)
        for hd in range(N_HEADS_LIN):
            o = o + jnp.where(vlane == hd, av[hd * c:(hd + 1) * c], 0.0)
        upd = _dot(v.T.astype(BF16), (kk * eu).astype(BF16))
        st_s[...] = (st * eb[c - 1:c, :] + upd) * bd_ref[...]
        out = _gated_head_norm(o, _pad_rows(og_ref[rows, :], c), gg_ref[...])
        o_ref[rows, :] = out[:c_in]

    _for_chunks(chunk, cps, c_in)

    @pl.when(si == n_steps - 1)
    def _():
        sout_ref[0] = st_s[...]


def _scan_call(kernel_fn, name, h, tok_cols, consts_before, s0k, consts_after, n_seq, seq, **static):
    c = SCAN_C
    c_in = min(seq, c)
    cps = max(min(seq, SCAN_ROWS_PER_STEP) // c, 1)
    n_steps = seq // (c_in * cps)
    rows = c_in * cps
    tables = _scan_tables(c)
    tok = [pl.BlockSpec((rows, w), lambda n, si, col=col: (n * n_steps + si, col))
           for (w, col) in tok_cols]
    cst = lambda x: pl.BlockSpec(x.shape, lambda n, si, nd=x.ndim: (0,) * nd)
    state_shape = s0k.shape[1:]
    state = pl.BlockSpec((1,) + state_shape, lambda n, si: (n,) + (0,) * len(state_shape))
    consts_after = list(consts_after) + list(tables)
    return pl.pallas_call(
        functools.partial(kernel_fn, c=c, c_in=c_in, cps=cps, n_steps=n_steps, **static),
        grid=(n_seq, n_steps),
        in_specs=tok + [cst(x) for x in consts_before] + [state] + [cst(x) for x in consts_after],
        out_specs=[pl.BlockSpec((rows, LIN_VW), lambda n, si: (n * n_steps + si, 0)), state],
        out_shape=[jax.ShapeDtypeStruct((n_seq * seq, LIN_VW), F32),
                   jax.ShapeDtypeStruct((n_seq,) + state_shape, F32)],
        scratch_shapes=[pltpu.VMEM(state_shape, F32)],
        compiler_params=pltpu.CompilerParams(
            dimension_semantics=("arbitrary", "arbitrary"),
            vmem_limit_bytes=_vmem_limit(48 << 20)),
        name=name,
    )(*([h] * len(tok_cols)), *consts_before, s0k, *consts_after)


def _hgrn_scan(h, lb_logits, g_hgrn, s0, layer, n_seq, seq):
    s0t = jnp.swapaxes(s0, 2, 3)
    zero = jnp.zeros_like(s0t)
    even = jnp.concatenate([s0t, zero], axis=2)
    odd = jnp.concatenate([zero, s0t], axis=2)
    is_odd = (jnp.arange(N_HEADS_LIN) % 2 == 1)[None, :, None, None]
    s0k = jnp.where(is_odd, odd, even)
    npair = N_HEADS_LIN // 2

    def pack(x):
        x = x.reshape(n_seq, npair, 2, LANES, HGRN_K)
        return jnp.transpose(x, (0, 1, 3, 2, 4)).reshape(n_seq, npair, LANES, 2 * HGRN_K)

    def unpack(x):
        x = x.reshape(n_seq, npair, LANES, 2, HGRN_K)
        return jnp.transpose(x, (0, 1, 3, 2, 4)).reshape(n_seq, N_HEADS_LIN, LANES, HGRN_K)
    tok_cols = [(HGRN_KW, 3), (HGRN_KW, 4), (LIN_VW, 10), (LIN_VW, 11)]
    o, sk = _scan_call(_hgrn_kernel, "hgrn_scan", h, tok_cols, [lb_logits, g_hgrn], pack(s0k), [],
                       n_seq, seq, layer=layer)
    sk = unpack(sk).reshape(n_seq, N_HEADS_LIN, 2, LIN_V, HGRN_K)
    s_even, s_odd = sk[:, :, 0], sk[:, :, 1]
    st = jnp.where(is_odd, s_odd, s_even)
    return o, jnp.swapaxes(st, 2, 3)


def _gla_scan(h, wg, bg, g_gla, s0, n_seq, seq):
    bd = (np.arange(LIN_VW)[:, None] // LIN_V == np.arange(GLA_KW)[None, :] // GLA_K)
    bd = jnp.asarray(bd.astype(np.float32))
    s0t = jnp.swapaxes(s0, 2, 3)
    eye = jnp.eye(N_HEADS_LIN, dtype=F32)[None, :, None, :, None]
    s0k = (s0t[:, :, :, None, :] * eye).reshape(n_seq, LIN_VW, GLA_KW)
    tok_cols = [(GLA_KW, 24), (GLA_KW, 25), (LIN_VW, 13), (LANES, 30), (LIN_VW, 14)]
    o, sk = _scan_call(_gla_kernel, "gla_scan", h, tok_cols, [wg, bg, g_gla], s0k, [bd], n_seq, seq)
    st = jnp.stack([sk[:, hd * LIN_V:(hd + 1) * LIN_V, hd * GLA_K:(hd + 1) * GLA_K]
                    for hd in range(N_HEADS_LIN)], axis=1)
    return o, jnp.swapaxes(st, 2, 3)


def _gelu_tanh(x):
    return x * (0.5 * (1.0 + jnp.tanh(math.sqrt(2.0 / math.pi) * (x + 0.044715 * (x * x * x)))))


FFN_CHUNK = 256
FFN_SLABS = 8


def _post_kernel(*refs, tm, seq_rows, tiles_per_seq):
    per_tile_seqs = seq_rows < tm
    if per_tile_seqs:
        (oa_ref, ob_ref, oc_ref, x_ref, wout_ref, gpm_ref, gpf_ref, wup_ref, wconv_ref,
         wdown_ref, gpo_ref, s1_ref, s2_ref, xo_ref, u_ref, slab_s, halo_s) = refs
    else:
        (oa_ref, ob_ref, oc_ref, x_ref, wout_ref, gpm_ref, gpf_ref, wup_ref, wconv_ref,
         wdown_ref, gpo_ref, prev_ref, xo_ref, conv_ref, slab_s, halo_s) = refs
    i = pl.program_id(0)
    w2 = 2 * D_FF
    h0 = SUBLANES
    mix = jnp.concatenate([oa_ref[...], ob_ref[...], oc_ref[...]], axis=1).astype(BF16)
    x1 = x_ref[...] + _rms(_dot(mix, wout_ref[...]), gpm_ref[...])
    xn = _rms(x1, gpf_ref[...]).astype(BF16)
    if per_tile_seqs:
        halo_s[...] = jnp.zeros((h0, w2), F32)
        t = lax.broadcasted_iota(jnp.int32, (tm, 1), 0) % seq_rows
    else:
        @pl.when(i % tiles_per_seq == 0)
        def _():
            halo_s[0:h0 - 2, :] = jnp.zeros((h0 - 2, w2), F32)
            halo_s[h0 - 2:h0, :] = prev_ref[0]

    def conv_tile(u_tile, col, ti):
        cols = slice(col, col + LANES)
        slab_s[ti, 0:h0, :] = halo_s[:, cols]
        slab_s[ti, h0:h0 + tm, :] = u_tile
        sh1 = slab_s[ti, h0 - 1:h0 - 1 + tm, :]
        sh2 = slab_s[ti, h0 - 2:h0 - 2 + tm, :]
        if per_tile_seqs:
            sh1 = jnp.where(t >= 1, sh1, s1_ref[:, cols])
            sh2 = jnp.where(t >= 2, sh2, s2_ref[:, cols])
            u_ref[:, cols] = u_tile
        else:
            conv_ref[0, :, cols] = u_tile[tm - 2:tm]
            halo_s[:, cols] = u_tile[tm - h0:tm]
        return sh2 * wconv_ref[0:1, cols] + sh1 * wconv_ref[1:2, cols] + u_tile * wconv_ref[2:3, cols]

    ffn = jnp.zeros((tm, D_MODEL), F32)
    tiles = FFN_CHUNK // LANES
    for ci, c0 in enumerate(range(0, D_FF, FFN_CHUNK)):
        parts = []
        for pi, base in enumerate((c0, D_FF + c0)):
            u = _dot(xn, wup_ref[:, base:base + FFN_CHUNK])
            slab0 = ((ci % 2) * 2 + pi) * tiles
            parts.append(jnp.concatenate(
                [conv_tile(u[:, k * LANES:(k + 1) * LANES], base + k * LANES, slab0 + k)
                 for k in range(tiles)], axis=1))
        act = (_gelu_tanh(parts[0]) * parts[1]).astype(BF16)
        ffn = ffn + _dot(act, wdown_ref[c0:c0 + FFN_CHUNK, :])
    xo_ref[...] = x1 + _rms(ffn, gpo_ref[...])


def _post_mix_ffn(oa, ob, oc, x, wout, gpm, gpf, wup, wconv, wdown, gpo, conv_prev, n_seq, seq, tm):
    m = x.shape[0]
    w2 = 2 * D_FF
    per_tile_seqs = seq < tm
    tiles_per_seq = max(seq // tm, 1)
    row = lambda w: pl.BlockSpec((tm, w), lambda i: (i, 0))
    cst = lambda shape: pl.BlockSpec(shape, lambda i, nd=len(shape): (0,) * nd,
                                     pipeline_mode=pl.Buffered(1))
    in_specs = [row(ATT_W), row(LIN_VW), row(LIN_VW), row(D_MODEL),
                cst((D_MODEL, D_MODEL)), cst((1, D_MODEL)), cst((1, D_MODEL)),
                cst((D_MODEL, w2)), cst((CONV_W, w2)), cst((D_FF, D_MODEL)), cst((1, D_MODEL))]
    args = [oa, ob, oc, x, wout, gpm, gpf, wup, wconv, wdown, gpo]
    if per_tile_seqs:
        s1 = jnp.pad(conv_prev[:, 1:2], ((0, 0), (0, seq - 1), (0, 0)))
        s2 = jnp.pad(conv_prev, ((0, 0), (0, seq - 2), (0, 0)))
        in_specs += [row(w2), row(w2)]
        args += [s1.reshape(m, w2), s2.reshape(m, w2)]
        second = pl.BlockSpec((tm, w2), lambda i: (i, 0))
        second_shape = jax.ShapeDtypeStruct((m, w2), F32)
    else:
        in_specs += [pl.BlockSpec((1, CONV_W - 1, w2), lambda i: (i // tiles_per_seq, 0, 0))]
        args += [conv_prev]
        second = pl.BlockSpec((1, CONV_W - 1, w2), lambda i: (i // tiles_per_seq, 0, 0))
        second_shape = jax.ShapeDtypeStruct((n_seq, CONV_W - 1, w2), F32)
    est = ((D_MODEL * D_MODEL + D_MODEL * w2 + D_FF * D_MODEL) * 2
           + 6 * tm * w2 * 4 + 8 * tm * D_MODEL * 4 + (6 << 20))
    return pl.pallas_call(
        functools.partial(_post_kernel, tm=tm, seq_rows=seq, tiles_per_seq=tiles_per_seq),
        grid=(m // tm,),
        in_specs=in_specs,
        out_specs=[row(D_MODEL), second],
        out_shape=[jax.ShapeDtypeStruct((m, D_MODEL), F32), second_shape],
        scratch_shapes=[pltpu.VMEM((FFN_SLABS, tm + SUBLANES, LANES), F32),
                        pltpu.VMEM((SUBLANES, w2), F32)],
        compiler_params=pltpu.CompilerParams(
            dimension_semantics=("arbitrary",), vmem_limit_bytes=_vmem_limit(est)),
        name="post_mix_ffn",
    )(*args)


def _permute_w_in(w):
    r0 = IN_COLS - LIN_VW - GLA_RANK
    pad = jnp.zeros((w.shape[0], IN_COLS_PAD - IN_COLS), w.dtype)
    return jnp.concatenate([w[:, :r0], w[:, r0 + GLA_RANK:], w[:, r0:r0 + GLA_RANK], pad],
                           axis=1).astype(BF16)


def kernel(x_prompt, x_sample, cache_win_k, cache_win_v, state_hgrn, state_gla, state_conv,
           w_in, w_gate_up, b_gate_up, w_out, g_hgrn, g_gla, lb_logits,
           w_up, w_conv, w_down, g_pre_mix, g_post_mix, g_pre_ffn, g_post_ffn):
    nb, seq, d = x_prompt.shape
    db, t_new, _ = x_sample.shape
    win = cache_win_k.shape[2]
    tm = 256
    slopes = jnp.asarray(2.0 ** (-8.0 * np.arange(1, N_HEADS_ATT + 1) / N_HEADS_ATT), F32)
    ck = jnp.transpose(cache_win_k, (0, 1, 3, 4, 2)).reshape(N_LAYERS, db, ATT_W, win)
    cv = jnp.transpose(cache_win_v, (0, 1, 3, 4, 2)).reshape(N_LAYERS, db, ATT_W, win)

    xp = x_prompt.reshape(nb * seq, d)
    xs = x_sample.reshape(db * t_new, d)
    zero_hgrn = jnp.zeros((nb, N_HEADS_LIN, HGRN_K, LIN_V), F32)
    zero_gla = jnp.zeros((nb, N_HEADS_LIN, GLA_K, LIN_V), F32)
    zero_conv = jnp.zeros((nb, CONV_W - 1, 2 * D_FF), F32)
    keep = min(WIN_MAX, seq)

    wk_buf = wv_buf = None
    wkp, wvp, hg_p, hg_s, gl_p, gl_s, cv_p, cv_s = [], [], [], [], [], [], [], []
    for l in range(N_LAYERS):
        w_in_l = _permute_w_in(w_in[l])
        wg = jnp.zeros((LANES, GLA_KW), F32).at[:GLA_RANK].set(w_gate_up[l]).astype(BF16)
        bg = b_gate_up[l].reshape(1, GLA_KW)
        gpre = g_pre_mix[l].reshape(1, d)
        gh = g_hgrn[l].reshape(1, LIN_VW)
        gg = g_gla[l].reshape(1, LIN_VW)
        post_w = (w_out[l].astype(BF16), g_post_mix[l].reshape(1, d), g_pre_ffn[l].reshape(1, d),
                  w_up[l].astype(BF16), w_conv[l], w_down[l].astype(BF16), g_post_ffn[l].reshape(1, d))

        hp = _norm_matmul(xp, gpre, w_in_l, tm)
        oa, k_win, v_win = _attn_prompt(hp, slopes, nb, seq, keep)
        ob, sh = _hgrn_scan(hp, lb_logits, gh, zero_hgrn, l, nb, seq)
        oc, sg = _gla_scan(hp, wg, bg, gg, zero_gla, nb, seq)
        xp, conv_p = _post_mix_ffn(oa, ob, oc, xp, *post_w, zero_conv, nb, seq, 2 * tm)
        wkp.append(k_win)
        wvp.append(v_win)
        hg_p.append(sh)
        gl_p.append(sg)
        cv_p.append(conv_p)

        hs = _norm_matmul(xs, gpre, w_in_l, db * t_new)
        oa, wk_buf, wv_buf = _attn_decode(hs, ck, cv, wk_buf, wv_buf, l, db, t_new, win)
        ob, sh = _hgrn_scan(hs, lb_logits, gh, state_hgrn[l], l, db, t_new)
        oc, sg = _gla_scan(hs, wg, bg, gg, state_gla[l], db, t_new)
        xs, u_s = _post_mix_ffn(oa, ob, oc, xs, *post_w, state_conv[l], db, t_new, db * t_new)
        hg_s.append(sh)
        gl_s.append(sg)
        cv_s.append(u_s.reshape(db, t_new, 2 * D_FF)[:, t_new - (CONV_W - 1):])

    hshape = (N_LAYERS, nb, keep, N_HEADS_ATT, HEAD_DIM)

    def token_major(w):
        w = w.reshape(N_LAYERS, db, N_HEADS_ATT, HEAD_DIM, win)
        return jnp.transpose(w, (0, 1, 4, 2, 3))
    return (xp.reshape(nb, seq, d), xs.reshape(db, t_new, d),
            jnp.stack(wkp).reshape(hshape), jnp.stack(wvp).reshape(hshape),
            token_major(wk_buf), token_major(wv_buf),
            jnp.stack(hg_p), jnp.stack(hg_s), jnp.stack(gl_p), jnp.stack(gl_s),
            jnp.stack(cv_p), jnp.stack(cv_s))
```

```python
import functools
import math

import numpy as np
import jax
import jax.numpy as jnp
from jax import lax
from jax.experimental import pallas as pl
from jax.experimental.pallas import tpu as pltpu

F32 = jnp.float32
BF16 = jnp.bfloat16

D_MODEL = 1024
N_LAYERS = 4
HEAD_DIM = 64
N_HEADS_ATT = 8
ATT_W = N_HEADS_ATT * HEAD_DIM
ATT_J = 128
ATT_DILATIONS = (1, 4, 16)
WIN_MAX = 2048
ATT_UNROLL = 8
ATT_DECODE_PAIRS = 2
N_HEADS_LIN = 4
HGRN_K = 128
GLA_K = 32
LIN_V = 64
LIN_VW = N_HEADS_LIN * LIN_V
HGRN_KW = N_HEADS_LIN * HGRN_K
GLA_KW = N_HEADS_LIN * GLA_K
GLA_RANK = 16
GLA_GATE_NORM = 16.0
D_FF = 2816
CONV_W = 3
RMS_EPS = 1e-6
IN_COLS = 3856
IN_COLS_PAD = 3968

LANES = 128
SUBLANES = 8
VMEM_LIMIT_CAP = 56 * 1024 * 1024

SCAN_C = 128
NEG_BIG = 1e30
LOG2E = 1.4426950408889634
SCAN_ROWS_PER_STEP = 512


def _vmem_limit(nbytes):
    return int(min(VMEM_LIMIT_CAP, max(32 * 1024 * 1024, nbytes)))


def _rms(x, g):
    ms = jnp.mean(x * x, axis=-1, keepdims=True)
    return x * lax.rsqrt(ms + RMS_EPS) * g


def _dot(a, b):
    return jnp.dot(a, b, preferred_element_type=F32)


def _dot_nt(a, b):
    return lax.dot_general(a, b, (((1,), (1,)), ((), ())), preferred_element_type=F32)


def _norm_matmul_kernel(x_ref, g_ref, w_ref, o_ref):
    xn = _rms(x_ref[...], g_ref[...]).astype(BF16)
    o_ref[...] = _dot(xn, w_ref[...])


def _norm_matmul(x, g, w, tm):
    m, d = x.shape
    n = w.shape[1]
    est = 2 * tm * d * 4 + 2 * d * n * 2 + 2 * tm * n * 4 + (4 << 20)
    return pl.pallas_call(
        _norm_matmul_kernel,
        grid=(m // tm,),
        in_specs=[
            pl.BlockSpec((tm, d), lambda i: (i, 0)),
            pl.BlockSpec((1, d), lambda i: (0, 0)),
            pl.BlockSpec((d, n), lambda i: (0, 0)),
        ],
        out_specs=pl.BlockSpec((tm, n), lambda i: (i, 0)),
        out_shape=jax.ShapeDtypeStruct((m, n), F32),
        compiler_params=pltpu.CompilerParams(
            dimension_semantics=("arbitrary",), vmem_limit_bytes=_vmem_limit(est)),
        name="norm_matmul",
    )(x, g, w)


def _attn_prompt_kernel(slopes_ref, q_ref, k_ref, v_ref, o_ref, kw_ref, vw_ref, m_s, l_s, acc_s, bias_s,
                        *, seq, keep):
    hp = pl.program_id(1)
    kw_ref[...] = k_ref[seq - keep:seq, :].T
    vw_ref[...] = v_ref[seq - keep:seq, :].T
    j = ATT_J
    qi = lax.broadcasted_iota(jnp.int32, (j, 2 * j), 0)
    ki = lax.broadcasted_iota(jnp.int32, (j, 2 * j), 1)
    d1 = j + qi - ki
    dt1 = jnp.where((d1 >= 0) & (d1 <= j), d1.astype(F32), NEG_BIG)
    d0 = qi - ki
    dt0 = jnp.where((ki < j) & (d0 >= 0), d0.astype(F32), NEG_BIG)
    for g, dil in enumerate(ATT_DILATIONS):
        for half in (0, 1):
            sc = slopes_ref[2 * hp + half] * (float(dil) * LOG2E)
            bias_s[g, half, 0] = -sc * dt0
            bias_s[g, half, 1] = -sc * dt1
    lane = lax.broadcasted_iota(jnp.int32, (1, LANES), 1)
    half1 = lane >= HEAD_DIM
    ones = jnp.ones((2 * j, LANES), BF16)

    def branch(g, dil, init):
        nb = seq // (j * dil)
        lognb = int(math.log2(nb))
        unroll = ATT_UNROLL
        assert unroll % nb == 0 or nb % unroll == 0

        def rows(start, size):
            return pl.ds(start, size) if dil == 1 else pl.ds(start, size, stride=dil)

        def load(idx, u):
            b = idx & (nb - 1)
            r = idx >> lognb
            start = b * (j * dil) + r
            if unroll % nb == 0:
                first = u % nb == 0
                kstart = r if first else start - j * dil
                sel = 0 if first else 1
            elif u == 0:
                first = b == 0
                kstart = jnp.where(first, r, start - j * dil)
                sel = jnp.where(first, 0, 1)
            else:
                kstart = start - j * dil
                sel = 1
            if dil == 1:
                start = pl.multiple_of(start, j)
                kstart = pl.multiple_of(kstart, j)
            q = q_ref[rows(start, j), :] * (HEAD_DIM ** -0.5 * LOG2E)
            kk = k_ref[rows(kstart, 2 * j), :].astype(BF16)
            vv = v_ref[rows(kstart, 2 * j), :].astype(BF16)
            old = None
            if not init:
                old = (m_s[0, rows(start, j), :], m_s[1, rows(start, j), :],
                       l_s[0, rows(start, j), :], l_s[1, rows(start, j), :],
                       acc_s[rows(start, j), :])
            return start, sel, q, kk, vv, old

        def compute(sel, q, kk, vv, old):
            vx = jnp.concatenate([vv, ones], axis=1)
            m_out, l_out, acc_new = [], [], None
            for half in (0, 1):
                hmask = half1 if half else jnp.logical_not(half1)
                qh = jnp.where(hmask, q, 0.0).astype(BF16)
                s = _dot_nt(qh, kk) + bias_s[g, half, sel]
                mx = jnp.max(s, axis=-1, keepdims=True)
                if init:
                    m_new = jnp.broadcast_to(mx, (j, LANES))
                else:
                    m_new = jnp.maximum(old[half], mx)
                p = jnp.exp2(s - jnp.concatenate([m_new, m_new], axis=1)).astype(BF16)
                pvx = _dot(p, vx)
                pv, rs = pvx[:, :LANES], pvx[:, LANES:]
                if init:
                    l_new, contrib = rs, pv
                else:
                    alpha = jnp.exp2(old[half] - m_new)
                    l_new = alpha * old[2 + half] + rs
                    contrib = alpha * old[4] + pv
                m_out.append(m_new)
                l_out.append(l_new)
                acc_new = contrib if half == 0 else jnp.where(half1, contrib, acc_new)
            return m_out, l_out, acc_new

        def body(it, carry):
            loaded = [load(it * unroll + u, u) for u in range(unroll)]
            done = [(ld[0],) + compute(*ld[1:]) for ld in loaded]
            for start, m_out, l_out, acc_new in done:
                for half in (0, 1):
                    m_s[half, rows(start, j), :] = m_out[half]
                    l_s[half, rows(start, j), :] = l_out[half]
                acc_s[rows(start, j), :] = acc_new
            return carry

        lax.fori_loop(0, nb * dil // unroll, body, 0)

    order = sorted(range(len(ATT_DILATIONS)), key=lambda g: -ATT_DILATIONS[g])
    for n, g in enumerate(order):
        branch(g, ATT_DILATIONS[g], n == 0)

    def finish(c, carry):
        rows = pl.ds(pl.multiple_of(c * j, j), j)
        l = jnp.where(half1, l_s[1, rows, :], l_s[0, rows, :])
        o_ref[rows, :] = acc_s[rows, :] / l
        return carry

    lax.fori_loop(0, seq // j, finish, 0)


def _attn_prompt(h, slopes, n_seq, seq, keep):
    npairs = ATT_W // LANES
    blk = (seq, LANES)
    wblk = pl.BlockSpec((None, LANES, keep), lambda n, p: (n, p, 0))
    wshape = jax.ShapeDtypeStruct((n_seq, ATT_W, keep), F32)
    est = 5 * 2 * seq * LANES * 4 + 5 * seq * LANES * 4 + (8 << 20)
    return pl.pallas_call(
        functools.partial(_attn_prompt_kernel, seq=seq, keep=keep),
        grid=(n_seq, npairs),
        in_specs=[
            pl.BlockSpec(memory_space=pltpu.SMEM),
            pl.BlockSpec(blk, lambda n, p: (n, p)),
            pl.BlockSpec(blk, lambda n, p: (n, npairs + p)),
            pl.BlockSpec(blk, lambda n, p: (n, 2 * npairs + p)),
        ],
        out_specs=[pl.BlockSpec(blk, lambda n, p: (n, p)), wblk, wblk],
        out_shape=[jax.ShapeDtypeStruct((n_seq * seq, ATT_W), F32), wshape, wshape],
        scratch_shapes=[
            pltpu.VMEM((2, seq, LANES), F32),
            pltpu.VMEM((2, seq, LANES), F32),
            pltpu.VMEM((seq, LANES), F32),
            pltpu.VMEM((len(ATT_DILATIONS), 2, 2, ATT_J, 2 * ATT_J), F32),
        ],
        compiler_params=pltpu.CompilerParams(
            dimension_semantics=("arbitrary", "arbitrary"), vmem_limit_bytes=_vmem_limit(est)),
        name="attn_prompt",
    )(slopes, h, h, h)


def _attn_decode_tables(t_new, win):
    cols = win + LANES
    c = np.arange(cols)[None, :]
    i = np.arange(t_new)[:, None]
    delta = win + i - c
    ok = (delta >= 0) & (c < win + t_new)
    mult = np.zeros((t_new, cols), np.float32)
    for dil in ATT_DILATIONS:
        mult += (ok & (delta % dil == 0) & (delta <= ATT_J * dil)).astype(np.float32)
    slopes = 2.0 ** (-8.0 * np.arange(1, N_HEADS_ATT + 1) / N_HEADS_ATT)
    bias = -slopes[:, None, None] * delta[None].astype(np.float64)
    bias = np.where(mult[None] > 0, bias, -NEG_BIG).astype(np.float32)
    bias = bias.reshape(N_HEADS_ATT * t_new, cols)
    mult = np.tile(mult, (N_HEADS_ATT, 1))
    return bias, mult


def _attn_decode_kernel(bias_ref, mult_ref, q_ref, k_ref, v_ref, ck_ref, cv_ref,
                        o_ref, wk_ref, wv_ref, *, t_new, win):
    for pp in range(ATT_DECODE_PAIRS):
        cols = slice(pp * LANES, (pp + 1) * LANES)
        rows = slice(pp * 2 * t_new, (pp + 1) * 2 * t_new)
        _attn_decode_pair(bias_ref.at[rows, :], mult_ref.at[rows, :], q_ref.at[:, cols], k_ref.at[:, cols],
                          v_ref.at[:, cols], ck_ref.at[cols, :], cv_ref.at[cols, :], o_ref.at[:, cols],
                          wk_ref.at[cols, :], wv_ref.at[cols, :], t_new=t_new, win=win)


def _attn_decode_pair(bias_ref, mult_ref, q_ref, k_ref, v_ref, ck_ref, cv_ref,
                      o_ref, wk_ref, wv_ref, *, t_new, win):
    lane = lax.broadcasted_iota(jnp.int32, (1, LANES), 1)
    half1 = lane >= HEAD_DIM
    q = q_ref[...] * (HEAD_DIM ** -0.5)
    qs = jnp.concatenate([jnp.where(half1, 0.0, q), jnp.where(half1, q, 0.0)], axis=0).astype(BF16)
    pad = jnp.zeros((LANES - t_new, LANES), F32)
    kt_new = jnp.concatenate([k_ref[...], pad], axis=0).T
    vt_new = jnp.concatenate([v_ref[...], pad], axis=0).T
    ck = ck_ref[...]
    cv = cv_ref[...]
    s = jnp.concatenate([_dot(qs, ck.astype(BF16)), _dot(qs, kt_new.astype(BF16))], axis=1)
    s = s + bias_ref[...]
    mx = jnp.max(s, axis=-1, keepdims=True)
    p = mult_ref[...] * jnp.exp(s - mx)
    l = jnp.sum(p, axis=-1, keepdims=True)
    pb = p.astype(BF16)
    o_all = (_dot_nt(pb[:, :win], cv.astype(BF16)) + _dot_nt(pb[:, win:], vt_new.astype(BF16))) / l
    o_ref[...] = jnp.where(half1, o_all[t_new:], o_all[:t_new])
    for c_ref, new, w_ref in ((ck_ref, kt_new, wk_ref), (cv_ref, vt_new, wv_ref)):
        shifted = pltpu.roll(c_ref[...], win - t_new, axis=1)
        tail = jnp.where(lane >= LANES - t_new, pltpu.roll(new, LANES - t_new, axis=1),
                         shifted[:, win - LANES:])
        w_ref[:, 0:win - LANES] = shifted[:, 0:win - LANES]
        w_ref[:, win - LANES:] = tail


def _attn_decode(h, cache_k, cache_v, wk_buf, wv_buf, layer, n_seq, t_new, win):
    bias, mult = _attn_decode_tables(t_new, win)
    pp = ATT_DECODE_PAIRS
    nsteps = ATT_W // (pp * LANES)
    rows = pp * 2 * t_new
    cols = win + LANES
    tab = pl.BlockSpec((rows, cols), lambda n, p: (p, 0))
    tok = lambda c: pl.BlockSpec((t_new, pp * LANES), lambda n, p, c=c: (n, c * nsteps + p))
    cache = pl.BlockSpec((None, None, pp * LANES, win), lambda n, p: (layer, n, p, 0))
    in_specs = [tab, tab, tok(0), tok(1), tok(2), cache, cache]
    args = [jnp.asarray(bias), jnp.asarray(mult), h, h, h, cache_k, cache_v]
    aliases = {}
    if wk_buf is not None:
        in_specs += [pl.BlockSpec(memory_space=pl.ANY), pl.BlockSpec(memory_space=pl.ANY)]
        args += [wk_buf, wv_buf]
        aliases = {7: 1, 8: 2}
    kernel = functools.partial(_attn_decode_kernel, t_new=t_new, win=win)
    if wk_buf is not None:
        kernel = lambda *refs, _k=kernel: _k(*refs[:7], *refs[9:])
    win_shape = jax.ShapeDtypeStruct((N_LAYERS, n_seq, ATT_W, win), F32)
    est = 16 * pp * LANES * win * 4 + (8 << 20)
    return pl.pallas_call(
        kernel,
        grid=(n_seq, nsteps),
        in_specs=in_specs,
        out_specs=[pl.BlockSpec((t_new, pp * LANES), lambda n, p: (n, p)), cache, cache],
        out_shape=[jax.ShapeDtypeStruct((n_seq * t_new, ATT_W), F32), win_shape, win_shape],
        input_output_aliases=aliases,
        compiler_params=pltpu.CompilerParams(
            dimension_semantics=("arbitrary", "arbitrary"), vmem_limit_bytes=_vmem_limit(est)),
        name="attn_decode",
    )(*args)


SCAN_MXU_LEVELS = 3


def _scan_tables(c):
    levels = int(math.log2(c))
    t = np.arange(c)[:, None]
    u = np.arange(c)[None, :]
    mats, masks = [], [np.eye(c, dtype=np.float32)]
    for li in range(levels):
        m = 1 << li
        bnd = (t // (2 * m)) * 2 * m + m - 1
        right = (t % (2 * m)) >= m
        if li < SCAN_MXU_LEVELS:
            mats.append(np.where(right, (u > bnd) & (u <= t), (u > t) & (u <= bnd)))
        masks.append(((t // (2 * m)) == (u // (2 * m))) & right & ((u % (2 * m)) < m))
    small = np.concatenate(mats, axis=0).astype(np.float32)
    tri = (u <= t).astype(np.float32)
    return (jnp.asarray(np.concatenate([small, small], axis=1), BF16),
            jnp.asarray(np.concatenate([tri, tri, tri], axis=1), BF16),
            jnp.asarray(np.stack(masks).astype(np.float32)))


def _scan_exponents(g, small_ref, tri_ref, c, levels):
    g1 = g.astype(BF16)
    r1 = g - g1.astype(F32)
    g2 = r1.astype(BF16)
    g3 = (r1 - g2.astype(F32)).astype(BF16)
    n_small = min(levels, SCAN_MXU_LEVELS)
    small = jnp.exp2(_dot(small_ref[0:n_small * c, :], jnp.concatenate([g1, g2], axis=0)))
    e = [small[li * c:(li + 1) * c] for li in range(n_small)]
    b = _dot(tri_ref[...], jnp.concatenate([g1, g2, g3], axis=0))
    row = lax.broadcasted_iota(jnp.int32, (c, 1), 0)
    for li in range(SCAN_MXU_LEVELS, levels):
        m = 1 << li
        bnd = jnp.concatenate(
            [jnp.broadcast_to(b[blk * 2 * m + m - 1:blk * 2 * m + m], (2 * m, g.shape[1]))
             for blk in range(c // (2 * m))], axis=0)
        sign = jnp.where((row & (2 * m - 1)) >= m, 1.0, -1.0)
        e.append(jnp.exp2((b - bnd) * sign))
    return e, jnp.exp2(b), jnp.exp2(b[c - 1:c] - b)


def _stack_heads(x, qmasks):
    if qmasks is None:
        return x.astype(BF16)
    return jnp.concatenate([jnp.where(m, x, 0.0) for m in qmasks], axis=0).astype(BF16)


def _scan_scores(qs, kk, e, masks_ref, c, qmasks=None):
    row = lax.broadcasted_iota(jnp.int32, (c, 1), 0)
    reps = 1 if qmasks is None else len(qmasks)

    def mask(i):
        m = masks_ref[i]
        return m if reps == 1 else jnp.concatenate([m] * reps, axis=0)

    a = mask(0) * _dot_nt(_stack_heads(qs, qmasks), kk.astype(BF16))
    for li in range(len(e)):
        m = 1 << li
        right = (row & (2 * m - 1)) >= m
        w = jnp.where(right, qs, kk) * e[li]
        a = a + mask(li + 1) * _dot_nt(_stack_heads(w, qmasks), w.astype(BF16))
    return a


def _pad_rows(x, c):
    if x.shape[0] == c:
        return x
    return jnp.concatenate([x, jnp.zeros((c - x.shape[0], x.shape[1]), x.dtype)], axis=0)


def _gated_head_norm(o, gate, g):
    lane = lax.broadcasted_iota(jnp.int32, (1, LANES), 1)
    half1 = lane >= LIN_V
    o2 = o * o
    parts = []
    for p in range(LIN_VW // LANES):
        o2p = o2[:, p * LANES:(p + 1) * LANES]
        s_all = jnp.sum(o2p, axis=-1, keepdims=True)
        s_hi = jnp.sum(jnp.where(half1, o2p, 0.0), axis=-1, keepdims=True)
        parts.append(jnp.where(half1, s_hi, s_all - s_hi))
    ms = jnp.concatenate(parts, axis=1) * (1.0 / LIN_V)
    on = o * lax.rsqrt(ms + RMS_EPS)
    return on * g * (gate * (1.0 / (1.0 + jnp.exp(-gate))))


def _hgrn_kernel(q_ref, f_ref, i_ref, og_ref, lbl_ref, gh_ref, s0_ref, small_ref, tri_ref, masks_ref,
                 o_ref, sout_ref, st_s, *, layer, c, c_in, cps, n_steps):
    si = pl.program_id(1)

    @pl.when(si == 0)
    def _():
        st_s[...] = s0_ref[0]

    lbl = lbl_ref[...]
    ex = jnp.exp(lbl - jnp.max(lbl, axis=0, keepdims=True))
    prob = ex / jnp.sum(ex, axis=0, keepdims=True)
    lb = jnp.zeros((1, HGRN_KW), F32)
    for r in range(1, layer + 1):
        lb = lb + prob[r:r + 1]
    lane = lax.broadcasted_iota(jnp.int32, (1, LANES), 1)
    half1 = lane >= LIN_V
    lane2 = lax.broadcasted_iota(jnp.int32, (1, 2 * HGRN_K), 1)
    pair_masks = [lane2 < HGRN_K, lane2 >= HGRN_K]
    levels = max(int(math.log2(c_in)), 1)

    def chunk(rows):
        q = _pad_rows(q_ref[rows, :], c)
        f = _pad_rows(f_ref[rows, :], c)
        vi = _pad_rows(i_ref[rows, :], c)
        forget = lb + (1.0 - lb) * (1.0 / (1.0 + jnp.exp(-f)))
        g = jnp.log2(forget)
        k = 1.0 - forget
        if c_in < c:
            valid = lax.broadcasted_iota(jnp.int32, (c, 1), 0) < c_in
            g = jnp.where(valid, g, 0.0)
            k = jnp.where(valid, k, 0.0)
        pairs = []
        for p in range(N_HEADS_LIN // 2):
            v = vi[:, p * LANES:(p + 1) * LANES]
            cols = slice(p * 2 * HGRN_K, (p + 1) * 2 * HGRN_K)
            qs = q[:, cols] * (HGRN_K ** -0.5)
            kk = k[:, cols]
            e, eb, eu = _scan_exponents(g[:, cols], small_ref, tri_ref, c, levels)
            a = _scan_scores(qs, kk, e, masks_ref, c, qmasks=pair_masks)
            st = st_s[p]
            o2 = (_dot(a.astype(BF16), v.astype(BF16))
                  + _dot_nt(_stack_heads(qs * eb, pair_masks), st.astype(BF16)))
            st_s[p] = st * eb[c - 1:c, :] + _dot(v.T.astype(BF16), (kk * eu).astype(BF16))
            pairs.append(jnp.where(half1, o2[c:], o2[:c]))
        o = jnp.concatenate(pairs, axis=1)
        out = _gated_head_norm(o, _pad_rows(og_ref[rows, :], c), gh_ref[...])
        o_ref[rows, :] = out[:c_in]

    _for_chunks(chunk, cps, c_in)

    @pl.when(si == n_steps - 1)
    def _():
        sout_ref[0] = st_s[...]


def _for_chunks(chunk, cps, c_in):
    if cps == 1:
        chunk(slice(None))
    else:
        def body(ci, carry):
            chunk(pl.ds(pl.multiple_of(ci * c_in, c_in), c_in))
            return carry
        lax.fori_loop(0, cps, body, 0)


def _gla_kernel(q_ref, k_ref, v_ref, r_ref, og_ref, wg_ref, bg_ref, gg_ref, s0_ref, bd_ref,
                small_ref, tri_ref, masks_ref, o_ref, sout_ref, st_s, *, c, c_in, cps, n_steps):
    si = pl.program_id(1)

    @pl.when(si == 0)
    def _():
        st_s[...] = s0_ref[0]

    klane = lax.broadcasted_iota(jnp.int32, (1, GLA_KW), 1) // GLA_K
    vlane = lax.broadcasted_iota(jnp.int32, (1, LIN_VW), 1) // LIN_V

    def chunk(rows):
        qs = _pad_rows(q_ref[rows, :], c) * (GLA_K ** -0.5)
        kk = _pad_rows(k_ref[rows, :], c)
        v = _pad_rows(v_ref[rows, :], c)
        x = _dot(_pad_rows(r_ref[rows, :], c).astype(BF16), wg_ref[...]) + bg_ref[...]
        g = (jnp.minimum(x, 0.0) - jnp.log(1.0 + jnp.exp(-jnp.abs(x)))) * (LOG2E / GLA_GATE_NORM)
        if c_in < c:
            valid = lax.broadcasted_iota(jnp.int32, (c, 1), 0) < c_in
            g = jnp.where(valid, g, 0.0)
            kk = jnp.where(valid, kk, 0.0)
        e, eb, eu = _scan_exponents(g, small_ref, tri_ref, c, max(int(math.log2(c_in)), 1))
        st = st_s[...]
        vb = v.astype(BF16)
        o = _dot_nt((qs * eb).astype(BF16), st.astype(BF16))
        a = _scan_scores(qs, kk, e, masks_ref, c, qmasks=[klane == hd for hd in range(N_HEADS_LIN)])
        av = _dot(a.astype(BF16), vb)
        for hd in range(N_HEADS_LIN):
            o = o + jnp.where(vlane == hd, av[hd * c:(hd + 1) * c], 0.0)
        upd = _dot(v.T.astype(BF16), (kk * eu).astype(BF16))
        st_s[...] = (st * eb[c - 1:c, :] + upd) * bd_ref[...]
        out = _gated_head_norm(o, _pad_rows(og_ref[rows, :], c), gg_ref[...])
        o_ref[rows, :] = out[:c_in]

    _for_chunks(chunk, cps, c_in)

    @pl.when(si == n_steps - 1)
    def _():
        sout_ref[0] = st_s[...]


def _scan_call(kernel_fn, name, h, tok_cols, consts_before, s0k, consts_after, n_seq, seq, **static):
    c = SCAN_C
    c_in = min(seq, c)
    cps = max(min(seq, SCAN_ROWS_PER_STEP) // c, 1)
    n_steps = seq // (c_in * cps)
    rows = c_in * cps
    tables = _scan_tables(c)
    tok = [pl.BlockSpec((rows, w), lambda n, si, col=col: (n * n_steps + si, col))
           for (w, col) in tok_cols]
    cst = lambda x: pl.BlockSpec(x.shape, lambda n, si, nd=x.ndim: (0,) * nd)
    state_shape = s0k.shape[1:]
    state = pl.BlockSpec((1,) + state_shape, lambda n, si: (n,) + (0,) * len(state_shape))
    consts_after = list(consts_after) + list(tables)
    return pl.pallas_call(
        functools.partial(kernel_fn, c=c, c_in=c_in, cps=cps, n_steps=n_steps, **static),
        grid=(n_seq, n_steps),
        in_specs=tok + [cst(x) for x in consts_before] + [state] + [cst(x) for x in consts_after],
        out_specs=[pl.BlockSpec((rows, LIN_VW), lambda n, si: (n * n_steps + si, 0)), state],
        out_shape=[jax.ShapeDtypeStruct((n_seq * seq, LIN_VW), F32),
                   jax.ShapeDtypeStruct((n_seq,) + state_shape, F32)],
        scratch_shapes=[pltpu.VMEM(state_shape, F32)],
        compiler_params=pltpu.CompilerParams(
            dimension_semantics=("arbitrary", "arbitrary"),
            vmem_limit_bytes=_vmem_limit(48 << 20)),
        name=name,
    )(*([h] * len(tok_cols)), *consts_before, s0k, *consts_after)


def _hgrn_scan(h, lb_logits, g_hgrn, s0, layer, n_seq, seq):
    s0t = jnp.swapaxes(s0, 2, 3)
    zero = jnp.zeros_like(s0t)
    even = jnp.concatenate([s0t, zero], axis=2)
    odd = jnp.concatenate([zero, s0t], axis=2)
    is_odd = (jnp.arange(N_HEADS_LIN) % 2 == 1)[None, :, None, None]
    s0k = jnp.where(is_odd, odd, even)
    npair = N_HEADS_LIN // 2

    def pack(x):
        x = x.reshape(n_seq, npair, 2, LANES, HGRN_K)
        return jnp.transpose(x, (0, 1, 3, 2, 4)).reshape(n_seq, npair, LANES, 2 * HGRN_K)

    def unpack(x):
        x = x.reshape(n_seq, npair, LANES, 2, HGRN_K)
        return jnp.transpose(x, (0, 1, 3, 2, 4)).reshape(n_seq, N_HEADS_LIN, LANES, HGRN_K)
    tok_cols = [(HGRN_KW, 3), (HGRN_KW, 4), (LIN_VW, 10), (LIN_VW, 11)]
    o, sk = _scan_call(_hgrn_kernel, "hgrn_scan", h, tok_cols, [lb_logits, g_hgrn], pack(s0k), [],
                       n_seq, seq, layer=layer)
    sk = unpack(sk).reshape(n_seq, N_HEADS_LIN, 2, LIN_V, HGRN_K)
    s_even, s_odd = sk[:, :, 0], sk[:, :, 1]
    st = jnp.where(is_odd, s_odd, s_even)
    return o, jnp.swapaxes(st, 2, 3)


def _gla_scan(h, wg, bg, g_gla, s0, n_seq, seq):
    bd = (np.arange(LIN_VW)[:, None] // LIN_V == np.arange(GLA_KW)[None, :] // GLA_K)
    bd = jnp.asarray(bd.astype(np.float32))
    s0t = jnp.swapaxes(s0, 2, 3)
    eye = jnp.eye(N_HEADS_LIN, dtype=F32)[None, :, None, :, None]
    s0k = (s0t[:, :, :, None, :] * eye).reshape(n_seq, LIN_VW, GLA_KW)
    tok_cols = [(GLA_KW, 24), (GLA_KW, 25), (LIN_VW, 13), (LANES, 30), (LIN_VW, 14)]
    o, sk = _scan_call(_gla_kernel, "gla_scan", h, tok_cols, [wg, bg, g_gla], s0k, [bd], n_seq, seq)
    st = jnp.stack([sk[:, hd * LIN_V:(hd + 1) * LIN_V, hd * GLA_K:(hd + 1) * GLA_K]
                    for hd in range(N_HEADS_LIN)], axis=1)
    return o, jnp.swapaxes(st, 2, 3)


def _gelu_tanh(x):
    return x * (0.5 * (1.0 + jnp.tanh(math.sqrt(2.0 / math.pi) * (x + 0.044715 * (x * x * x)))))


FFN_CHUNK = 256
FFN_SLABS = 8


def _post_kernel(*refs, tm, seq_rows, tiles_per_seq):
    per_tile_seqs = seq_rows < tm
    if per_tile_seqs:
        (oa_ref, ob_ref, oc_ref, x_ref, wout_ref, gpm_ref, gpf_ref, wup_ref, wconv_ref,
         wdown_ref, gpo_ref, s1_ref, s2_ref, xo_ref, u_ref, slab_s, halo_s) = refs
    else:
        (oa_ref, ob_ref, oc_ref, x_ref, wout_ref, gpm_ref, gpf_ref, wup_ref, wconv_ref,
         wdown_ref, gpo_ref, prev_ref, xo_ref, conv_ref, slab_s, halo_s) = refs
    i = pl.program_id(0)
    w2 = 2 * D_FF
    h0 = SUBLANES
    mix = jnp.concatenate([oa_ref[...], ob_ref[...], oc_ref[...]], axis=1).astype(BF16)
    x1 = x_ref[...] + _rms(_dot(mix, wout_ref[...]), gpm_ref[...])
    xn = _rms(x1, gpf_ref[...]).astype(BF16)
    if per_tile_seqs:
        halo_s[...] = jnp.zeros((h0, w2), F32)
        t = lax.broadcasted_iota(jnp.int32, (tm, 1), 0) % seq_rows
    else:
        @pl.when(i % tiles_per_seq == 0)
        def _():
            halo_s[0:h0 - 2, :] = jnp.zeros((h0 - 2, w2), F32)
            halo_s[h0 - 2:h0, :] = prev_ref[0]

    def conv_tile(u_tile, col, ti):
        cols = slice(col, col + LANES)
        slab_s[ti, 0:h0, :] = halo_s[:, cols]
        slab_s[ti, h0:h0 + tm, :] = u_tile
        sh1 = slab_s[ti, h0 - 1:h0 - 1 + tm, :]
        sh2 = slab_s[ti, h0 - 2:h0 - 2 + tm, :]
        if per_tile_seqs:
            sh1 = jnp.where(t >= 1, sh1, s1_ref[:, cols])
            sh2 = jnp.where(t >= 2, sh2, s2_ref[:, cols])
            u_ref[:, cols] = u_tile
        else:
            conv_ref[0, :, cols] = u_tile[tm - 2:tm]
            halo_s[:, cols] = u_tile[tm - h0:tm]
        return sh2 * wconv_ref[0:1, cols] + sh1 * wconv_ref[1:2, cols] + u_tile * wconv_ref[2:3, cols]

    ffn = jnp.zeros((tm, D_MODEL), F32)
    tiles = FFN_CHUNK // LANES
    for ci, c0 in enumerate(range(0, D_FF, FFN_CHUNK)):
        parts = []
        for pi, base in enumerate((c0, D_FF + c0)):
            u = _dot(xn, wup_ref[:, base:base + FFN_CHUNK])
            slab0 = ((ci % 2) * 2 + pi) * tiles
            parts.append(jnp.concatenate(
                [conv_tile(u[:, k * LANES:(k + 1) * LANES], base + k * LANES, slab0 + k)
                 for k in range(tiles)], axis=1))
        act = (_gelu_tanh(parts[0]) * parts[1]).astype(BF16)
        ffn = ffn + _dot(act, wdown_ref[c0:c0 + FFN_CHUNK, :])
    xo_ref[...] = x1 + _rms(ffn, gpo_ref[...])


def _post_mix_ffn(oa, ob, oc, x, wout, gpm, gpf, wup, wconv, wdown, gpo, conv_prev, n_seq, seq, tm):
    m = x.shape[0]
    w2 = 2 * D_FF
    per_tile_seqs = seq < tm
    tiles_per_seq = max(seq // tm, 1)
    row = lambda w: pl.BlockSpec((tm, w), lambda i: (i, 0))
    cst = lambda shape: pl.BlockSpec(shape, lambda i, nd=len(shape): (0,) * nd,
                                     pipeline_mode=pl.Buffered(1))
    in_specs = [row(ATT_W), row(LIN_VW), row(LIN_VW), row(D_MODEL),
                cst((D_MODEL, D_MODEL)), cst((1, D_MODEL)), cst((1, D_MODEL)),
                cst((D_MODEL, w2)), cst((CONV_W, w2)), cst((D_FF, D_MODEL)), cst((1, D_MODEL))]
    args = [oa, ob, oc, x, wout, gpm, gpf, wup, wconv, wdown, gpo]
    if per_tile_seqs:
        s1 = jnp.pad(conv_prev[:, 1:2], ((0, 0), (0, seq - 1), (0, 0)))
        s2 = jnp.pad(conv_prev, ((0, 0), (0, seq - 2), (0, 0)))
        in_specs += [row(w2), row(w2)]
        args += [s1.reshape(m, w2), s2.reshape(m, w2)]
        second = pl.BlockSpec((tm, w2), lambda i: (i, 0))
        second_shape = jax.ShapeDtypeStruct((m, w2), F32)
    else:
        in_specs += [pl.BlockSpec((1, CONV_W - 1, w2), lambda i: (i // tiles_per_seq, 0, 0))]
        args += [conv_prev]
        second = pl.BlockSpec((1, CONV_W - 1, w2), lambda i: (i // tiles_per_seq, 0, 0))
        second_shape = jax.ShapeDtypeStruct((n_seq, CONV_W - 1, w2), F32)
    est = ((D_MODEL * D_MODEL + D_MODEL * w2 + D_FF * D_MODEL) * 2
           + 6 * tm * w2 * 4 + 8 * tm * D_MODEL * 4 + (6 << 20))
    return pl.pallas_call(
        functools.partial(_post_kernel, tm=tm, seq_rows=seq, tiles_per_seq=tiles_per_seq),
        grid=(m // tm,),
        in_specs=in_specs,
        out_specs=[row(D_MODEL), second],
        out_shape=[jax.ShapeDtypeStruct((m, D_MODEL), F32), second_shape],
        scratch_shapes=[pltpu.VMEM((FFN_SLABS, tm + SUBLANES, LANES), F32),
                        pltpu.VMEM((SUBLANES, w2), F32)],
        compiler_params=pltpu.CompilerParams(
            dimension_semantics=("arbitrary",), vmem_limit_bytes=_vmem_limit(est)),
        name="post_mix_ffn",
    )(*args)


def _permute_w_in(w):
    r0 = IN_COLS - LIN_VW - GLA_RANK
    pad = jnp.zeros((w.shape[0], IN_COLS_PAD - IN_COLS), w.dtype)
    return jnp.concatenate([w[:, :r0], w[:, r0 + GLA_RANK:], w[:, r0:r0 + GLA_RANK], pad],
                           axis=1).astype(BF16)


def kernel(x_prompt, x_sample, cache_win_k, cache_win_v, state_hgrn, state_gla, state_conv,
           w_in, w_gate_up, b_gate_up, w_out, g_hgrn, g_gla, lb_logits,
           w_up, w_conv, w_down, g_pre_mix, g_post_mix, g_pre_ffn, g_post_ffn):
    nb, seq, d = x_prompt.shape
    db, t_new, _ = x_sample.shape
    win = cache_win_k.shape[2]
    tm = 256
    slopes = jnp.asarray(2.0 ** (-8.0 * np.arange(1, N_HEADS_ATT + 1) / N_HEADS_ATT), F32)
    ck = jnp.transpose(cache_win_k, (0, 1, 3, 4, 2)).reshape(N_LAYERS, db, ATT_W, win)
    cv = jnp.transpose(cache_win_v, (0, 1, 3, 4, 2)).reshape(N_LAYERS, db, ATT_W, win)

    xp = x_prompt.reshape(nb * seq, d)
    xs = x_sample.reshape(db * t_new, d)
    zero_hgrn = jnp.zeros((nb, N_HEADS_LIN, HGRN_K, LIN_V), F32)
    zero_gla = jnp.zeros((nb, N_HEADS_LIN, GLA_K, LIN_V), F32)
    zero_conv = jnp.zeros((nb, CONV_W - 1, 2 * D_FF), F32)
    keep = min(WIN_MAX, seq)

    wk_buf = wv_buf = None
    wkp, wvp, hg_p, hg_s, gl_p, gl_s, cv_p, cv_s = [], [], [], [], [], [], [], []
    for l in range(N_LAYERS):
        w_in_l = _permute_w_in(w_in[l])
        wg = jnp.zeros((LANES, GLA_KW), F32).at[:GLA_RANK].set(w_gate_up[l]).astype(BF16)
        bg = b_gate_up[l].reshape(1, GLA_KW)
        gpre = g_pre_mix[l].reshape(1, d)
        gh = g_hgrn[l].reshape(1, LIN_VW)
        gg = g_gla[l].reshape(1, LIN_VW)
        post_w = (w_out[l].astype(BF16), g_post_mix[l].reshape(1, d), g_pre_ffn[l].reshape(1, d),
                  w_up[l].astype(BF16), w_conv[l], w_down[l].astype(BF16), g_post_ffn[l].reshape(1, d))

        hp = _norm_matmul(xp, gpre, w_in_l, tm)
        oa, k_win, v_win = _attn_prompt(hp, slopes, nb, seq, keep)
        ob, sh = _hgrn_scan(hp, lb_logits, gh, zero_hgrn, l, nb, seq)
        oc, sg = _gla_scan(hp, wg, bg, gg, zero_gla, nb, seq)
        xp, conv_p = _post_mix_ffn(oa, ob, oc, xp, *post_w, zero_conv, nb, seq, 2 * tm)
        wkp.append(k_win)
        wvp.append(v_win)
        hg_p.append(sh)
        gl_p.append(sg)
        cv_p.append(conv_p)

        hs = _norm_matmul(xs, gpre, w_in_l, db * t_new)
        oa, wk_buf, wv_buf = _attn_decode(hs, ck, cv, wk_buf, wv_buf, l, db, t_new, win)
        ob, sh = _hgrn_scan(hs, lb_logits, gh, state_hgrn[l], l, db, t_new)
        oc, sg = _gla_scan(hs, wg, bg, gg, state_gla[l], db, t_new)
        xs, u_s = _post_mix_ffn(oa, ob, oc, xs, *post_w, state_conv[l], db, t_new, db * t_new)
        hg_s.append(sh)
        gl_s.append(sg)
        cv_s.append(u_s.reshape(db, t_new, 2 * D_FF)[:, t_new - (CONV_W - 1):])

    def token_major(w):
        w = w.reshape(N_LAYERS, w.shape[1], N_HEADS_ATT, HEAD_DIM, w.shape[3])
        return jnp.transpose(w, (0, 1, 4, 2, 3))
    return (xp.reshape(nb, seq, d), xs.reshape(db, t_new, d),
            token_major(jnp.stack(wkp)), token_major(jnp.stack(wvp)),
            token_major(wk_buf), token_major(wv_buf),
            jnp.stack(hg_p), jnp.stack(hg_s), jnp.stack(gl_p), jnp.stack(gl_s),
            jnp.stack(cv_p), jnp.stack(cv_s))
```

```python
import functools
import math

import numpy as np
import jax
import jax.numpy as jnp
from jax import lax
from jax.experimental import pallas as pl
from jax.experimental.pallas import tpu as pltpu

F32 = jnp.float32
BF16 = jnp.bfloat16

D_MODEL = 1024
N_LAYERS = 4
HEAD_DIM = 64
N_HEADS_ATT = 8
ATT_W = N_HEADS_ATT * HEAD_DIM
ATT_J = 128
ATT_DILATIONS = (1, 4, 16)
WIN_MAX = 2048
ATT_UNROLL = 8
ATT_DECODE_PAIRS = 2
N_HEADS_LIN = 4
HGRN_K = 128
GLA_K = 32
LIN_V = 64
LIN_VW = N_HEADS_LIN * LIN_V
HGRN_KW = N_HEADS_LIN * HGRN_K
GLA_KW = N_HEADS_LIN * GLA_K
GLA_RANK = 16
GLA_GATE_NORM = 16.0
D_FF = 2816
CONV_W = 3
RMS_EPS = 1e-6
IN_COLS = 3856
IN_COLS_PAD = 3968

LANES = 128
SUBLANES = 8
VMEM_LIMIT_CAP = 56 * 1024 * 1024

SCAN_C = 128
NEG_BIG = 1e30
LOG2E = 1.4426950408889634
SCAN_ROWS_PER_STEP = 1024


def _vmem_limit(nbytes):
    return int(min(VMEM_LIMIT_CAP, max(32 * 1024 * 1024, nbytes)))


def _rms(x, g):
    ms = jnp.mean(x * x, axis=-1, keepdims=True)
    return x * lax.rsqrt(ms + RMS_EPS) * g


def _dot(a, b):
    return jnp.dot(a, b, preferred_element_type=F32)


def _dot_nt(a, b):
    return lax.dot_general(a, b, (((1,), (1,)), ((), ())), preferred_element_type=F32)


def _norm_matmul_kernel(x_ref, g_ref, w_ref, o_ref):
    xn = _rms(x_ref[...], g_ref[...]).astype(BF16)
    o_ref[...] = _dot(xn, w_ref[...])


def _norm_matmul(x, g, w, tm):
    m, d = x.shape
    n = w.shape[1]
    est = 2 * tm * d * 4 + 2 * d * n * 2 + 2 * tm * n * 4 + (4 << 20)
    return pl.pallas_call(
        _norm_matmul_kernel,
        grid=(m // tm,),
        in_specs=[
            pl.BlockSpec((tm, d), lambda i: (i, 0)),
            pl.BlockSpec((1, d), lambda i: (0, 0)),
            pl.BlockSpec((d, n), lambda i: (0, 0)),
        ],
        out_specs=pl.BlockSpec((tm, n), lambda i: (i, 0)),
        out_shape=jax.ShapeDtypeStruct((m, n), F32),
        compiler_params=pltpu.CompilerParams(
            dimension_semantics=("arbitrary",), vmem_limit_bytes=_vmem_limit(est)),
        name="norm_matmul",
    )(x, g, w)


def _attn_prompt_kernel(slopes_ref, q_ref, k_ref, v_ref, o_ref, kw_ref, vw_ref, m_s, l_s, acc_s, bias_s,
                        *, seq, keep):
    hp = pl.program_id(1)
    kw_ref[...] = k_ref[seq - keep:seq, :].T
    vw_ref[...] = v_ref[seq - keep:seq, :].T
    j = ATT_J
    qi = lax.broadcasted_iota(jnp.int32, (j, 2 * j), 0)
    ki = lax.broadcasted_iota(jnp.int32, (j, 2 * j), 1)
    d1 = j + qi - ki
    dt1 = jnp.where((d1 >= 0) & (d1 <= j), d1.astype(F32), NEG_BIG)
    d0 = qi - ki
    dt0 = jnp.where((ki < j) & (d0 >= 0), d0.astype(F32), NEG_BIG)
    for g, dil in enumerate(ATT_DILATIONS):
        for half in (0, 1):
            sc = slopes_ref[2 * hp + half] * (float(dil) * LOG2E)
            bias_s[g, half, 0] = -sc * dt0
            bias_s[g, half, 1] = -sc * dt1
    lane = lax.broadcasted_iota(jnp.int32, (1, LANES), 1)
    half1 = lane >= HEAD_DIM
    ones = jnp.ones((2 * j, LANES), BF16)

    def branch(g, dil, init):
        nb = seq // (j * dil)
        lognb = int(math.log2(nb))
        unroll = ATT_UNROLL
        assert unroll % nb == 0 or nb % unroll == 0

        def rows(start, size):
            return pl.ds(start, size) if dil == 1 else pl.ds(start, size, stride=dil)

        def load(idx, u):
            b = idx & (nb - 1)
            r = idx >> lognb
            start = b * (j * dil) + r
            if unroll % nb == 0:
                first = u % nb == 0
                kstart = r if first else start - j * dil
                sel = 0 if first else 1
            elif u == 0:
                first = b == 0
                kstart = jnp.where(first, r, start - j * dil)
                sel = jnp.where(first, 0, 1)
            else:
                kstart = start - j * dil
                sel = 1
            if dil == 1:
                start = pl.multiple_of(start, j)
                kstart = pl.multiple_of(kstart, j)
            q = q_ref[rows(start, j), :] * (HEAD_DIM ** -0.5 * LOG2E)
            kk = k_ref[rows(kstart, 2 * j), :].astype(BF16)
            vv = v_ref[rows(kstart, 2 * j), :].astype(BF16)
            old = None
            if not init:
                old = (m_s[0, rows(start, j), :], m_s[1, rows(start, j), :],
                       l_s[0, rows(start, j), :], l_s[1, rows(start, j), :],
                       acc_s[rows(start, j), :])
            return start, sel, q, kk, vv, old

        def compute(sel, q, kk, vv, old):
            vx = jnp.concatenate([vv, ones], axis=1)
            m_out, l_out, acc_new = [], [], None
            for half in (0, 1):
                hmask = half1 if half else jnp.logical_not(half1)
                qh = jnp.where(hmask, q, 0.0).astype(BF16)
                s = _dot_nt(qh, kk) + bias_s[g, half, sel]
                mx = jnp.max(s, axis=-1, keepdims=True)
                if init:
                    m_new = jnp.broadcast_to(mx, (j, LANES))
                else:
                    m_new = jnp.maximum(old[half], mx)
                p = jnp.exp2(s - jnp.concatenate([m_new, m_new], axis=1)).astype(BF16)
                pvx = _dot(p, vx)
                pv, rs = pvx[:, :LANES], pvx[:, LANES:]
                if init:
                    l_new, contrib = rs, pv
                else:
                    alpha = jnp.exp2(old[half] - m_new)
                    l_new = alpha * old[2 + half] + rs
                    contrib = alpha * old[4] + pv
                m_out.append(m_new)
                l_out.append(l_new)
                acc_new = contrib if half == 0 else jnp.where(half1, contrib, acc_new)
            return m_out, l_out, acc_new

        def body(it, carry):
            loaded = [load(it * unroll + u, u) for u in range(unroll)]
            done = [(ld[0],) + compute(*ld[1:]) for ld in loaded]
            for start, m_out, l_out, acc_new in done:
                for half in (0, 1):
                    m_s[half, rows(start, j), :] = m_out[half]
                    l_s[half, rows(start, j), :] = l_out[half]
                acc_s[rows(start, j), :] = acc_new
            return carry

        lax.fori_loop(0, nb * dil // unroll, body, 0)

    order = sorted(range(len(ATT_DILATIONS)), key=lambda g: -ATT_DILATIONS[g])
    for n, g in enumerate(order):
        branch(g, ATT_DILATIONS[g], n == 0)

    def finish(c, carry):
        rows = pl.ds(pl.multiple_of(c * j, j), j)
        l = jnp.where(half1, l_s[1, rows, :], l_s[0, rows, :])
        o_ref[rows, :] = acc_s[rows, :] / l
        return carry

    lax.fori_loop(0, seq // j, finish, 0)


def _attn_prompt(h, slopes, n_seq, seq, keep):
    npairs = ATT_W // LANES
    blk = (seq, LANES)
    wblk = pl.BlockSpec((None, LANES, keep), lambda n, p: (n, p, 0))
    wshape = jax.ShapeDtypeStruct((n_seq, ATT_W, keep), F32)
    est = 5 * 2 * seq * LANES * 4 + 5 * seq * LANES * 4 + (8 << 20)
    return pl.pallas_call(
        functools.partial(_attn_prompt_kernel, seq=seq, keep=keep),
        grid=(n_seq, npairs),
        in_specs=[
            pl.BlockSpec(memory_space=pltpu.SMEM),
            pl.BlockSpec(blk, lambda n, p: (n, p)),
            pl.BlockSpec(blk, lambda n, p: (n, npairs + p)),
            pl.BlockSpec(blk, lambda n, p: (n, 2 * npairs + p)),
        ],
        out_specs=[pl.BlockSpec(blk, lambda n, p: (n, p)), wblk, wblk],
        out_shape=[jax.ShapeDtypeStruct((n_seq * seq, ATT_W), F32), wshape, wshape],
        scratch_shapes=[
            pltpu.VMEM((2, seq, LANES), F32),
            pltpu.VMEM((2, seq, LANES), F32),
            pltpu.VMEM((seq, LANES), F32),
            pltpu.VMEM((len(ATT_DILATIONS), 2, 2, ATT_J, 2 * ATT_J), F32),
        ],
        compiler_params=pltpu.CompilerParams(
            dimension_semantics=("arbitrary", "arbitrary"), vmem_limit_bytes=_vmem_limit(est)),
        name="attn_prompt",
    )(slopes, h, h, h)


def _attn_decode_tables(t_new, win):
    cols = win + LANES
    c = np.arange(cols)[None, :]
    i = np.arange(t_new)[:, None]
    delta = win + i - c
    ok = (delta >= 0) & (c < win + t_new)
    mult = np.zeros((t_new, cols), np.float32)
    for dil in ATT_DILATIONS:
        mult += (ok & (delta % dil == 0) & (delta <= ATT_J * dil)).astype(np.float32)
    slopes = 2.0 ** (-8.0 * np.arange(1, N_HEADS_ATT + 1) / N_HEADS_ATT)
    bias = -slopes[:, None, None] * delta[None].astype(np.float64)
    bias = np.where(mult[None] > 0, bias, -NEG_BIG).astype(np.float32)
    bias = bias.reshape(N_HEADS_ATT * t_new, cols)
    mult = np.tile(mult, (N_HEADS_ATT, 1))
    return bias, mult


def _attn_decode_kernel(bias_ref, mult_ref, q_ref, k_ref, v_ref, ck_ref, cv_ref,
                        o_ref, wk_ref, wv_ref, *, t_new, win):
    for pp in range(ATT_DECODE_PAIRS):
        cols = slice(pp * LANES, (pp + 1) * LANES)
        rows = slice(pp * 2 * t_new, (pp + 1) * 2 * t_new)
        _attn_decode_pair(bias_ref.at[rows, :], mult_ref.at[rows, :], q_ref.at[:, cols], k_ref.at[:, cols],
                          v_ref.at[:, cols], ck_ref.at[cols, :], cv_ref.at[cols, :], o_ref.at[:, cols],
                          wk_ref.at[cols, :], wv_ref.at[cols, :], t_new=t_new, win=win)


def _attn_decode_pair(bias_ref, mult_ref, q_ref, k_ref, v_ref, ck_ref, cv_ref,
                      o_ref, wk_ref, wv_ref, *, t_new, win):
    lane = lax.broadcasted_iota(jnp.int32, (1, LANES), 1)
    half1 = lane >= HEAD_DIM
    q = q_ref[...] * (HEAD_DIM ** -0.5)
    qs = jnp.concatenate([jnp.where(half1, 0.0, q), jnp.where(half1, q, 0.0)], axis=0).astype(BF16)
    pad = jnp.zeros((LANES - t_new, LANES), F32)
    kt_new = jnp.concatenate([k_ref[...], pad], axis=0).T
    vt_new = jnp.concatenate([v_ref[...], pad], axis=0).T
    ck = ck_ref[...]
    cv = cv_ref[...]
    s = jnp.concatenate([_dot(qs, ck.astype(BF16)), _dot(qs, kt_new.astype(BF16))], axis=1)
    s = s + bias_ref[...]
    mx = jnp.max(s, axis=-1, keepdims=True)
    p = mult_ref[...] * jnp.exp(s - mx)
    l = jnp.sum(p, axis=-1, keepdims=True)
    pb = p.astype(BF16)
    o_all = (_dot_nt(pb[:, :win], cv.astype(BF16)) + _dot_nt(pb[:, win:], vt_new.astype(BF16))) / l
    o_ref[...] = jnp.where(half1, o_all[t_new:], o_all[:t_new])
    for c_ref, new, w_ref in ((ck_ref, kt_new, wk_ref), (cv_ref, vt_new, wv_ref)):
        shifted = pltpu.roll(c_ref[...], win - t_new, axis=1)
        tail = jnp.where(lane >= LANES - t_new, pltpu.roll(new, LANES - t_new, axis=1),
                         shifted[:, win - LANES:])
        w_ref[:, 0:win - LANES] = shifted[:, 0:win - LANES]
        w_ref[:, win - LANES:] = tail


def _attn_decode(h, cache_k, cache_v, wk_buf, wv_buf, layer, n_seq, t_new, win):
    bias, mult = _attn_decode_tables(t_new, win)
    pp = ATT_DECODE_PAIRS
    nsteps = ATT_W // (pp * LANES)
    rows = pp * 2 * t_new
    cols = win + LANES
    tab = pl.BlockSpec((rows, cols), lambda n, p: (p, 0))
    tok = lambda c: pl.BlockSpec((t_new, pp * LANES), lambda n, p, c=c: (n, c * nsteps + p))
    cache = pl.BlockSpec((None, None, pp * LANES, win), lambda n, p: (layer, n, p, 0))
    in_specs = [tab, tab, tok(0), tok(1), tok(2), cache, cache]
    args = [jnp.asarray(bias), jnp.asarray(mult), h, h, h, cache_k, cache_v]
    aliases = {}
    if wk_buf is not None:
        in_specs += [pl.BlockSpec(memory_space=pl.ANY), pl.BlockSpec(memory_space=pl.ANY)]
        args += [wk_buf, wv_buf]
        aliases = {7: 1, 8: 2}
    kernel = functools.partial(_attn_decode_kernel, t_new=t_new, win=win)
    if wk_buf is not None:
        kernel = lambda *refs, _k=kernel: _k(*refs[:7], *refs[9:])
    win_shape = jax.ShapeDtypeStruct((N_LAYERS, n_seq, ATT_W, win), F32)
    est = 16 * pp * LANES * win * 4 + (8 << 20)
    return pl.pallas_call(
        kernel,
        grid=(n_seq, nsteps),
        in_specs=in_specs,
        out_specs=[pl.BlockSpec((t_new, pp * LANES), lambda n, p: (n, p)), cache, cache],
        out_shape=[jax.ShapeDtypeStruct((n_seq * t_new, ATT_W), F32), win_shape, win_shape],
        input_output_aliases=aliases,
        compiler_params=pltpu.CompilerParams(
            dimension_semantics=("arbitrary", "arbitrary"), vmem_limit_bytes=_vmem_limit(est)),
        name="attn_decode",
    )(*args)


SCAN_MXU_LEVELS = 3


def _scan_tables(c):
    levels = int(math.log2(c))
    t = np.arange(c)[:, None]
    u = np.arange(c)[None, :]
    mats, masks = [], [np.eye(c, dtype=np.float32)]
    for li in range(levels):
        m = 1 << li
        bnd = (t // (2 * m)) * 2 * m + m - 1
        right = (t % (2 * m)) >= m
        if li < SCAN_MXU_LEVELS:
            mats.append(np.where(right, (u > bnd) & (u <= t), (u > t) & (u <= bnd)))
        masks.append(((t // (2 * m)) == (u // (2 * m))) & right & ((u % (2 * m)) < m))
    small = np.concatenate(mats, axis=0).astype(np.float32)
    tri = (u <= t).astype(np.float32)
    return (jnp.asarray(np.concatenate([small, small], axis=1), BF16),
            jnp.asarray(np.concatenate([tri, tri, tri], axis=1), BF16),
            jnp.asarray(np.stack(masks).astype(np.float32)))


def _scan_exponents(g, small_ref, tri_ref, c, levels):
    g1 = g.astype(BF16)
    r1 = g - g1.astype(F32)
    g2 = r1.astype(BF16)
    g3 = (r1 - g2.astype(F32)).astype(BF16)
    n_small = min(levels, SCAN_MXU_LEVELS)
    small = jnp.exp2(_dot(small_ref[0:n_small * c, :], jnp.concatenate([g1, g2], axis=0)))
    e = [small[li * c:(li + 1) * c] for li in range(n_small)]
    b = _dot(tri_ref[...], jnp.concatenate([g1, g2, g3], axis=0))
    row = lax.broadcasted_iota(jnp.int32, (c, 1), 0)
    for li in range(SCAN_MXU_LEVELS, levels):
        m = 1 << li
        bnd = jnp.concatenate(
            [jnp.broadcast_to(b[blk * 2 * m + m - 1:blk * 2 * m + m], (2 * m, g.shape[1]))
             for blk in range(c // (2 * m))], axis=0)
        sign = jnp.where((row & (2 * m - 1)) >= m, 1.0, -1.0)
        e.append(jnp.exp2((b - bnd) * sign))
    return e, jnp.exp2(b), jnp.exp2(b[c - 1:c] - b)


def _stack_heads(x, qmasks):
    if qmasks is None:
        return x.astype(BF16)
    return jnp.concatenate([jnp.where(m, x, 0.0) for m in qmasks], axis=0).astype(BF16)


def _scan_scores(qs, kk, e, masks_ref, c, qmasks=None):
    row = lax.broadcasted_iota(jnp.int32, (c, 1), 0)
    reps = 1 if qmasks is None else len(qmasks)

    def mask(i):
        m = masks_ref[i]
        return m if reps == 1 else jnp.concatenate([m] * reps, axis=0)

    a = mask(0) * _dot_nt(_stack_heads(qs, qmasks), kk.astype(BF16))
    for li in range(len(e)):
        m = 1 << li
        right = (row & (2 * m - 1)) >= m
        w = jnp.where(right, qs, kk) * e[li]
        a = a + mask(li + 1) * _dot_nt(_stack_heads(w, qmasks), w.astype(BF16))
    return a


def _pad_rows(x, c):
    if x.shape[0] == c:
        return x
    return jnp.concatenate([x, jnp.zeros((c - x.shape[0], x.shape[1]), x.dtype)], axis=0)


def _gated_head_norm(o, gate, g):
    lane = lax.broadcasted_iota(jnp.int32, (1, LANES), 1)
    half1 = lane >= LIN_V
    o2 = o * o
    parts = []
    for p in range(LIN_VW // LANES):
        o2p = o2[:, p * LANES:(p + 1) * LANES]
        s_all = jnp.sum(o2p, axis=-1, keepdims=True)
        s_hi = jnp.sum(jnp.where(half1, o2p, 0.0), axis=-1, keepdims=True)
        parts.append(jnp.where(half1, s_hi, s_all - s_hi))
    ms = jnp.concatenate(parts, axis=1) * (1.0 / LIN_V)
    on = o * lax.rsqrt(ms + RMS_EPS)
    return on * g * (gate * (1.0 / (1.0 + jnp.exp(-gate))))


def _hgrn_kernel(q_ref, f_ref, i_ref, og_ref, lbl_ref, gh_ref, s0_ref, small_ref, tri_ref, masks_ref,
                 o_ref, sout_ref, st_s, *, layer, c, c_in, cps, n_steps):
    si = pl.program_id(1)

    @pl.when(si == 0)
    def _():
        st_s[...] = s0_ref[0]

    lbl = lbl_ref[...]
    ex = jnp.exp(lbl - jnp.max(lbl, axis=0, keepdims=True))
    prob = ex / jnp.sum(ex, axis=0, keepdims=True)
    lb = jnp.zeros((1, HGRN_KW), F32)
    for r in range(1, layer + 1):
        lb = lb + prob[r:r + 1]
    lane = lax.broadcasted_iota(jnp.int32, (1, LANES), 1)
    half1 = lane >= LIN_V
    lane2 = lax.broadcasted_iota(jnp.int32, (1, 2 * HGRN_K), 1)
    pair_masks = [lane2 < HGRN_K, lane2 >= HGRN_K]
    levels = max(int(math.log2(c_in)), 1)

    def chunk(rows):
        q = _pad_rows(q_ref[rows, :], c)
        f = _pad_rows(f_ref[rows, :], c)
        vi = _pad_rows(i_ref[rows, :], c)
        forget = lb + (1.0 - lb) * (1.0 / (1.0 + jnp.exp(-f)))
        g = jnp.log2(forget)
        k = 1.0 - forget
        if c_in < c:
            valid = lax.broadcasted_iota(jnp.int32, (c, 1), 0) < c_in
            g = jnp.where(valid, g, 0.0)
            k = jnp.where(valid, k, 0.0)
        pairs = []
        for p in range(N_HEADS_LIN // 2):
            v = vi[:, p * LANES:(p + 1) * LANES]
            cols = slice(p * 2 * HGRN_K, (p + 1) * 2 * HGRN_K)
            qs = q[:, cols] * (HGRN_K ** -0.5)
            kk = k[:, cols]
            e, eb, eu = _scan_exponents(g[:, cols], small_ref, tri_ref, c, levels)
            a = _scan_scores(qs, kk, e, masks_ref, c, qmasks=pair_masks)
            st = st_s[p]
            o2 = (_dot(a.astype(BF16), v.astype(BF16))
                  + _dot_nt(_stack_heads(qs * eb, pair_masks), st.astype(BF16)))
            st_s[p] = st * eb[c - 1:c, :] + _dot(v.T.astype(BF16), (kk * eu).astype(BF16))
            pairs.append(jnp.where(half1, o2[c:], o2[:c]))
        o = jnp.concatenate(pairs, axis=1)
        out = _gated_head_norm(o, _pad_rows(og_ref[rows, :], c), gh_ref[...])
        o_ref[rows, :] = out[:c_in]

    _for_chunks(chunk, cps, c_in)

    @pl.when(si == n_steps - 1)
    def _():
        sout_ref[0] = st_s[...]


def _for_chunks(chunk, cps, c_in):
    if cps == 1:
        chunk(slice(None))
    else:
        def body(ci, carry):
            chunk(pl.ds(pl.multiple_of(ci * c_in, c_in), c_in))
            return carry
        lax.fori_loop(0, cps, body, 0)


def _gla_kernel(q_ref, k_ref, v_ref, r_ref, og_ref, wg_ref, bg_ref, gg_ref, s0_ref, bd_ref,
                small_ref, tri_ref, masks_ref, o_ref, sout_ref, st_s, *, c, c_in, cps, n_steps):
    si = pl.program_id(1)

    @pl.when(si == 0)
    def _():
        st_s[...] = s0_ref[0]

    klane = lax.broadcasted_iota(jnp.int32, (1, GLA_KW), 1) // GLA_K
    vlane = lax.broadcasted_iota(jnp.int32, (1, LIN_VW), 1) // LIN_V

    def chunk(rows):
        qs = _pad_rows(q_ref[rows, :], c) * (GLA_K ** -0.5)
        kk = _pad_rows(k_ref[rows, :], c)
        v = _pad_rows(v_ref[rows, :], c)
        x = _dot(_pad_rows(r_ref[rows, :], c).astype(BF16), wg_ref[...]) + bg_ref[...]
        g = (jnp.minimum(x, 0.0) - jnp.log(1.0 + jnp.exp(-jnp.abs(x)))) * (LOG2E / GLA_GATE_NORM)
        if c_in < c:
            valid = lax.broadcasted_iota(jnp.int32, (c, 1), 0) < c_in
            g = jnp.where(valid, g, 0.0)
            kk = jnp.where(valid, kk, 0.0)
        e, eb, eu = _scan_exponents(g, small_ref, tri_ref, c, max(int(math.log2(c_in)), 1))
        st = st_s[...]
        vb = v.astype(BF16)
        o = _dot_nt((qs * eb).astype(BF16), st.astype(BF16))
        a = _scan_scores(qs, kk, e, masks_ref, c, qmasks=[klane == hd for hd in range(N_HEADS_LIN)])
        av = _dot(a.astype(BF16), vb)
        for hd in range(N_HEADS_LIN):
            o = o + jnp.where(vlane == hd, av[hd * c:(hd + 1) * c], 0.0)
        upd = _dot(v.T.astype(BF16), (kk * eu).astype(BF16))
        st_s[...] = (st * eb[c - 1:c, :] + upd) * bd_ref[...]
        out = _gated_head_norm(o, _pad_rows(og_ref[rows, :], c), gg_ref[...])
        o_ref[rows, :] = out[:c_in]

    _for_chunks(chunk, cps, c_in)

    @pl.when(si == n_steps - 1)
    def _():
        sout_ref[0] = st_s[...]


def _scan_call(kernel_fn, name, h, tok_cols, consts_before, s0k, consts_after, n_seq, seq, **static):
    c = SCAN_C
    c_in = min(seq, c)
    cps = max(min(seq, SCAN_ROWS_PER_STEP) // c, 1)
    n_steps = seq // (c_in * cps)
    rows = c_in * cps
    tables = _scan_tables(c)
    tok = [pl.BlockSpec((rows, w), lambda n, si, col=col: (n * n_steps + si, col))
           for (w, col) in tok_cols]
    cst = lambda x: pl.BlockSpec(x.shape, lambda n, si, nd=x.ndim: (0,) * nd)
    state_shape = s0k.shape[1:]
    state = pl.BlockSpec((1,) + state_shape, lambda n, si: (n,) + (0,) * len(state_shape))
    consts_after = list(consts_after) + list(tables)
    return pl.pallas_call(
        functools.partial(kernel_fn, c=c, c_in=c_in, cps=cps, n_steps=n_steps, **static),
        grid=(n_seq, n_steps),
        in_specs=tok + [cst(x) for x in consts_before] + [state] + [cst(x) for x in consts_after],
        out_specs=[pl.BlockSpec((rows, LIN_VW), lambda n, si: (n * n_steps + si, 0)), state],
        out_shape=[jax.ShapeDtypeStruct((n_seq * seq, LIN_VW), F32),
                   jax.ShapeDtypeStruct((n_seq,) + state_shape, F32)],
        scratch_shapes=[pltpu.VMEM(state_shape, F32)],
        compiler_params=pltpu.CompilerParams(
            dimension_semantics=("arbitrary", "arbitrary"),
            vmem_limit_bytes=_vmem_limit(48 << 20)),
        name=name,
    )(*([h] * len(tok_cols)), *consts_before, s0k, *consts_after)


def _hgrn_scan(h, lb_logits, g_hgrn, s0, layer, n_seq, seq):
    s0t = jnp.swapaxes(s0, 2, 3)
    zero = jnp.zeros_like(s0t)
    even = jnp.concatenate([s0t, zero], axis=2)
    odd = jnp.concatenate([zero, s0t], axis=2)
    is_odd = (jnp.arange(N_HEADS_LIN) % 2 == 1)[None, :, None, None]
    s0k = jnp.where(is_odd, odd, even)
    npair = N_HEADS_LIN // 2

    def pack(x):
        x = x.reshape(n_seq, npair, 2, LANES, HGRN_K)
        return jnp.transpose(x, (0, 1, 3, 2, 4)).reshape(n_seq, npair, LANES, 2 * HGRN_K)

    def unpack(x):
        x = x.reshape(n_seq, npair, LANES, 2, HGRN_K)
        return jnp.transpose(x, (0, 1, 3, 2, 4)).reshape(n_seq, N_HEADS_LIN, LANES, HGRN_K)
    tok_cols = [(HGRN_KW, 3), (HGRN_KW, 4), (LIN_VW, 10), (LIN_VW, 11)]
    o, sk = _scan_call(_hgrn_kernel, "hgrn_scan", h, tok_cols, [lb_logits, g_hgrn], pack(s0k), [],
                       n_seq, seq, layer=layer)
    sk = unpack(sk).reshape(n_seq, N_HEADS_LIN, 2, LIN_V, HGRN_K)
    s_even, s_odd = sk[:, :, 0], sk[:, :, 1]
    st = jnp.where(is_odd, s_odd, s_even)
    return o, jnp.swapaxes(st, 2, 3)


def _gla_scan(h, wg, bg, g_gla, s0, n_seq, seq):
    bd = (np.arange(LIN_VW)[:, None] // LIN_V == np.arange(GLA_KW)[None, :] // GLA_K)
    bd = jnp.asarray(bd.astype(np.float32))
    s0t = jnp.swapaxes(s0, 2, 3)
    eye = jnp.eye(N_HEADS_LIN, dtype=F32)[None, :, None, :, None]
    s0k = (s0t[:, :, :, None, :] * eye).reshape(n_seq, LIN_VW, GLA_KW)
    tok_cols = [(GLA_KW, 24), (GLA_KW, 25), (LIN_VW, 13), (LANES, 30), (LIN_VW, 14)]
    o, sk = _scan_call(_gla_kernel, "gla_scan", h, tok_cols, [wg, bg, g_gla], s0k, [bd], n_seq, seq)
    st = jnp.stack([sk[:, hd * LIN_V:(hd + 1) * LIN_V, hd * GLA_K:(hd + 1) * GLA_K]
                    for hd in range(N_HEADS_LIN)], axis=1)
    return o, jnp.swapaxes(st, 2, 3)


def _gelu_tanh(x):
    return x * (0.5 * (1.0 + jnp.tanh(math.sqrt(2.0 / math.pi) * (x + 0.044715 * (x * x * x)))))


FFN_CHUNK = 256
FFN_SLABS = 8


def _post_kernel(*refs, tm, seq_rows, tiles_per_seq):
    per_tile_seqs = seq_rows < tm
    if per_tile_seqs:
        (oa_ref, ob_ref, oc_ref, x_ref, wout_ref, gpm_ref, gpf_ref, wup_ref, wconv_ref,
         wdown_ref, gpo_ref, s1_ref, s2_ref, xo_ref, u_ref, slab_s, halo_s) = refs
    else:
        (oa_ref, ob_ref, oc_ref, x_ref, wout_ref, gpm_ref, gpf_ref, wup_ref, wconv_ref,
         wdown_ref, gpo_ref, prev_ref, xo_ref, conv_ref, slab_s, halo_s) = refs
    i = pl.program_id(0)
    w2 = 2 * D_FF
    h0 = SUBLANES
    mix = jnp.concatenate([oa_ref[...], ob_ref[...], oc_ref[...]], axis=1).astype(BF16)
    x1 = x_ref[...] + _rms(_dot(mix, wout_ref[...]), gpm_ref[...])
    xn = _rms(x1, gpf_ref[...]).astype(BF16)
    if per_tile_seqs:
        halo_s[...] = jnp.zeros((h0, w2), F32)
        t = lax.broadcasted_iota(jnp.int32, (tm, 1), 0) % seq_rows
    else:
        @pl.when(i % tiles_per_seq == 0)
        def _():
            halo_s[0:h0 - 2, :] = jnp.zeros((h0 - 2, w2), F32)
            halo_s[h0 - 2:h0, :] = prev_ref[0]

    def conv_tile(u_tile, col, ti):
        cols = slice(col, col + LANES)
        slab_s[ti, 0:h0, :] = halo_s[:, cols]
        slab_s[ti, h0:h0 + tm, :] = u_tile
        sh1 = slab_s[ti, h0 - 1:h0 - 1 + tm, :]
        sh2 = slab_s[ti, h0 - 2:h0 - 2 + tm, :]
        if per_tile_seqs:
            sh1 = jnp.where(t >= 1, sh1, s1_ref[:, cols])
            sh2 = jnp.where(t >= 2, sh2, s2_ref[:, cols])
            u_ref[:, cols] = u_tile
        else:
            conv_ref[0, :, cols] = u_tile[tm - 2:tm]
            halo_s[:, cols] = u_tile[tm - h0:tm]
        return sh2 * wconv_ref[0:1, cols] + sh1 * wconv_ref[1:2, cols] + u_tile * wconv_ref[2:3, cols]

    ffn = jnp.zeros((tm, D_MODEL), F32)
    tiles = FFN_CHUNK // LANES
    for ci, c0 in enumerate(range(0, D_FF, FFN_CHUNK)):
        parts = []
        for pi, base in enumerate((c0, D_FF + c0)):
            u = _dot(xn, wup_ref[:, base:base + FFN_CHUNK])
            slab0 = ((ci % 2) * 2 + pi) * tiles
            parts.append(jnp.concatenate(
                [conv_tile(u[:, k * LANES:(k + 1) * LANES], base + k * LANES, slab0 + k)
                 for k in range(tiles)], axis=1))
        act = (_gelu_tanh(parts[0]) * parts[1]).astype(BF16)
        ffn = ffn + _dot(act, wdown_ref[c0:c0 + FFN_CHUNK, :])
    xo_ref[...] = x1 + _rms(ffn, gpo_ref[...])


def _post_mix_ffn(oa, ob, oc, x, wout, gpm, gpf, wup, wconv, wdown, gpo, conv_prev, n_seq, seq, tm):
    m = x.shape[0]
    w2 = 2 * D_FF
    per_tile_seqs = seq < tm
    tiles_per_seq = max(seq // tm, 1)
    row = lambda w: pl.BlockSpec((tm, w), lambda i: (i, 0))
    cst = lambda shape: pl.BlockSpec(shape, lambda i, nd=len(shape): (0,) * nd,
                                     pipeline_mode=pl.Buffered(1))
    in_specs = [row(ATT_W), row(LIN_VW), row(LIN_VW), row(D_MODEL),
                cst((D_MODEL, D_MODEL)), cst((1, D_MODEL)), cst((1, D_MODEL)),
                cst((D_MODEL, w2)), cst((CONV_W, w2)), cst((D_FF, D_MODEL)), cst((1, D_MODEL))]
    args = [oa, ob, oc, x, wout, gpm, gpf, wup, wconv, wdown, gpo]
    if per_tile_seqs:
        s1 = jnp.pad(conv_prev[:, 1:2], ((0, 0), (0, seq - 1), (0, 0)))
        s2 = jnp.pad(conv_prev, ((0, 0), (0, seq - 2), (0, 0)))
        in_specs += [row(w2), row(w2)]
        args += [s1.reshape(m, w2), s2.reshape(m, w2)]
        second = pl.BlockSpec((tm, w2), lambda i: (i, 0))
        second_shape = jax.ShapeDtypeStruct((m, w2), F32)
    else:
        in_specs += [pl.BlockSpec((1, CONV_W - 1, w2), lambda i: (i // tiles_per_seq, 0, 0))]
        args += [conv_prev]
        second = pl.BlockSpec((1, CONV_W - 1, w2), lambda i: (i // tiles_per_seq, 0, 0))
        second_shape = jax.ShapeDtypeStruct((n_seq, CONV_W - 1, w2), F32)
    est = ((D_MODEL * D_MODEL + D_MODEL * w2 + D_FF * D_MODEL) * 2
           + 6 * tm * w2 * 4 + 8 * tm * D_MODEL * 4 + (6 << 20))
    return pl.pallas_call(
        functools.partial(_post_kernel, tm=tm, seq_rows=seq, tiles_per_seq=tiles_per_seq),
        grid=(m // tm,),
        in_specs=in_specs,
        out_specs=[row(D_MODEL), second],
        out_shape=[jax.ShapeDtypeStruct((m, D_MODEL), F32), second_shape],
        scratch_shapes=[pltpu.VMEM((FFN_SLABS, tm + SUBLANES, LANES), F32),
                        pltpu.VMEM((SUBLANES, w2), F32)],
        compiler_params=pltpu.CompilerParams(
            dimension_semantics=("arbitrary",), vmem_limit_bytes=_vmem_limit(est)),
        name="post_mix_ffn",
    )(*args)


def _permute_w_in(w):
    r0 = IN_COLS - LIN_VW - GLA_RANK
    pad = jnp.zeros((w.shape[0], IN_COLS_PAD - IN_COLS), w.dtype)
    return jnp.concatenate([w[:, :r0], w[:, r0 + GLA_RANK:], w[:, r0:r0 + GLA_RANK], pad],
                           axis=1).astype(BF16)


def kernel(x_prompt, x_sample, cache_win_k, cache_win_v, state_hgrn, state_gla, state_conv,
           w_in, w_gate_up, b_gate_up, w_out, g_hgrn, g_gla, lb_logits,
           w_up, w_conv, w_down, g_pre_mix, g_post_mix, g_pre_ffn, g_post_ffn):
    nb, seq, d = x_prompt.shape
    db, t_new, _ = x_sample.shape
    win = cache_win_k.shape[2]
    tm = 256
    slopes = jnp.asarray(2.0 ** (-8.0 * np.arange(1, N_HEADS_ATT + 1) / N_HEADS_ATT), F32)
    ck = jnp.transpose(cache_win_k, (0, 1, 3, 4, 2)).reshape(N_LAYERS, db, ATT_W, win)
    cv = jnp.transpose(cache_win_v, (0, 1, 3, 4, 2)).reshape(N_LAYERS, db, ATT_W, win)

    xp = x_prompt.reshape(nb * seq, d)
    xs = x_sample.reshape(db * t_new, d)
    zero_hgrn = jnp.zeros((nb, N_HEADS_LIN, HGRN_K, LIN_V), F32)
    zero_gla = jnp.zeros((nb, N_HEADS_LIN, GLA_K, LIN_V), F32)
    zero_conv = jnp.zeros((nb, CONV_W - 1, 2 * D_FF), F32)
    keep = min(WIN_MAX, seq)

    wk_buf = wv_buf = None
    wkp, wvp, hg_p, hg_s, gl_p, gl_s, cv_p, cv_s = [], [], [], [], [], [], [], []
    for l in range(N_LAYERS):
        w_in_l = _permute_w_in(w_in[l])
        wg = jnp.zeros((LANES, GLA_KW), F32).at[:GLA_RANK].set(w_gate_up[l]).astype(BF16)
        bg = b_gate_up[l].reshape(1, GLA_KW)
        gpre = g_pre_mix[l].reshape(1, d)
        gh = g_hgrn[l].reshape(1, LIN_VW)
        gg = g_gla[l].reshape(1, LIN_VW)
        post_w = (w_out[l].astype(BF16), g_post_mix[l].reshape(1, d), g_pre_ffn[l].reshape(1, d),
                  w_up[l].astype(BF16), w_conv[l], w_down[l].astype(BF16), g_post_ffn[l].reshape(1, d))

        hp = _norm_matmul(xp, gpre, w_in_l, tm)
        oa, k_win, v_win = _attn_prompt(hp, slopes, nb, seq, keep)
        ob, sh = _hgrn_scan(hp, lb_logits, gh, zero_hgrn, l, nb, seq)
        oc, sg = _gla_scan(hp, wg, bg, gg, zero_gla, nb, seq)
        xp, conv_p = _post_mix_ffn(oa, ob, oc, xp, *post_w, zero_conv, nb, seq, 2 * tm)
        wkp.append(k_win)
        wvp.append(v_win)
        hg_p.append(sh)
        gl_p.append(sg)
        cv_p.append(conv_p)

        hs = _norm_matmul(xs, gpre, w_in_l, db * t_new)
        oa, wk_buf, wv_buf = _attn_decode(hs, ck, cv, wk_buf, wv_buf, l, db, t_new, win)
        ob, sh = _hgrn_scan(hs, lb_logits, gh, state_hgrn[l], l, db, t_new)
        oc, sg = _gla_scan(hs, wg, bg, gg, state_gla[l], db, t_new)
        xs, u_s = _post_mix_ffn(oa, ob, oc, xs, *post_w, state_conv[l], db, t_new, db * t_new)
        hg_s.append(sh)
        gl_s.append(sg)
        cv_s.append(u_s.reshape(db, t_new, 2 * D_FF)[:, t_new - (CONV_W - 1):])

    def token_major(w):
        w = w.reshape(N_LAYERS, w.shape[1], N_HEADS_ATT, HEAD_DIM, w.shape[3])
        return jnp.transpose(w, (0, 1, 4, 2, 3))
    return (xp.reshape(nb, seq, d), xs.reshape(db, t_new, d),
            token_major(jnp.stack(wkp)), token_major(jnp.stack(wvp)),
            token_major(wk_buf), token_major(wv_buf),
            jnp.stack(hg_p), jnp.stack(hg_s), jnp.stack(gl_p), jnp.stack(gl_s),
            jnp.stack(cv_p), jnp.stack(cv_s))
```
